```python
import jax, jax.numpy as jnp
from jax import lax
import numpy as np

D_MODEL = 2048
BATCH = 2
SEQ = 8192
DEPTH = 1
DEC_BATCH = 32
DEC_SEQ = 1
PAST_LEN = 16384
PAGE_SIZE = 128

HEAD_DIM = 128
N_HEADS_SB = 8
N_HEADS_DSA = 8
N_KV_DSA = 2
N_HEADS_IDX = 16
IDX_DIM = 64
TOPK_MAX = 256
D_FF = 5632
ROPE_THETA = 500000.0
ROT_FRACTION = 4
Q_BLOCK = 128
EPS = 1e-6
POOL_EXTRA_DIV = 4
SCALE = HEAD_DIM ** -0.5

W_SB = N_HEADS_SB * HEAD_DIM
W_DSA_Q = N_HEADS_DSA * HEAD_DIM
W_DSA_KV = N_KV_DSA * HEAD_DIM
W_IDX_Q = N_HEADS_IDX * IDX_DIM
SPLITS = (W_SB, W_SB, W_SB, W_DSA_Q, W_DSA_KV, W_DSA_KV, W_IDX_Q, IDX_DIM, N_HEADS_IDX, D_MODEL, D_MODEL)
D_IN = sum(SPLITS)

kernel_name = 'stick_breaking_dsa_hybrid_step'


def rms_norm(x, g):
    x32 = x.astype(jnp.float32)
    y = x32 * lax.rsqrt(jnp.mean(x32 * x32, axis=-1, keepdims=True) + EPS)
    return (y * g.astype(jnp.float32)).astype(x.dtype)


def swiglu(h, w1, w3, w2):
    return (jax.nn.silu(h @ w1) * (h @ w3)) @ w2


def macaron_half(x, g, w1, w3, w2):
    return x + 0.5 * swiglu(rms_norm(x, g), w1, w3, w2)


def rope_partial(x, pos):
    rot = x.shape[-1] // ROT_FRACTION
    half = rot // 2
    inv = 1.0 / (ROPE_THETA ** (jnp.arange(half, dtype=jnp.float32) * 2.0 / rot))
    ang = pos.astype(jnp.float32)[:, None] * inv[None, :]
    cos = jnp.cos(ang)[:, None, :]
    sin = jnp.sin(ang)[:, None, :]
    x32 = x.astype(jnp.float32)
    x1 = x32[..., :half]
    x2 = x32[..., half:rot]
    out = jnp.concatenate([x1 * cos - x2 * sin, x2 * cos + x1 * sin, x32[..., rot:]], axis=-1)
    return out.astype(x.dtype)


def mixer_inputs(h, pos, w_in):
    n, s, _ = h.shape
    offsets = [int(c) for c in np.cumsum(SPLITS)[:-1]]
    q_a, k_a, v_a, q_b, k_b, v_b, q_i, k_i, w_i, g_a, g_b = jnp.split(h @ w_in, offsets, axis=-1)
    q_a = q_a.reshape(n, s, N_HEADS_SB, HEAD_DIM)
    k_a = k_a.reshape(n, s, N_HEADS_SB, HEAD_DIM)
    v_a = v_a.reshape(n, s, N_HEADS_SB, HEAD_DIM)
    q_b = rope_partial(q_b.reshape(n, s, N_HEADS_DSA, HEAD_DIM), pos)
    k_b = rope_partial(k_b.reshape(n, s, N_KV_DSA, HEAD_DIM), pos)
    v_b = v_b.reshape(n, s, N_KV_DSA, HEAD_DIM)
    q_i = rope_partial(q_i.reshape(n, s, N_HEADS_IDX, IDX_DIM), pos)
    k_i = rope_partial(k_i.reshape(n, s, 1, IDX_DIM), pos)[:, :, 0]
    return q_a, k_a, v_a, q_b, k_b, v_b, q_i, k_i, w_i, g_a, g_b


def stick_breaking_weights(z, mask):
    log_beta = jax.nn.log_sigmoid(z)
    log_rest = jnp.where(mask, jax.nn.log_sigmoid(-z), 0.0)
    incl = lax.cumsum(log_rest, axis=z.ndim - 1, reverse=True)
    excl = jnp.concatenate([incl[..., 1:], jnp.zeros_like(incl[..., :1])], axis=-1)
    return jnp.where(mask, jnp.exp(log_beta + excl), 0.0)


def index_scores(q_i, w_i, k_i):
    logits = jnp.einsum('...qhd,...sd->...qhs', q_i, k_i)
    return jnp.einsum('...qhs,...qh->...qs', jax.nn.relu(logits), w_i).astype(jnp.float32)


def gathered_attention(q, kg, vg, valid):
    *lead, nq, _, dh = q.shape
    qg = q.reshape(*lead, nq, N_KV_DSA, N_HEADS_DSA // N_KV_DSA, dh)
    sc = jnp.einsum('...qgrd,...qkgd->...qgrk', qg, kg).astype(jnp.float32) * SCALE
    sc = jnp.where(valid[..., :, None, None, :], sc, -jnp.inf)
    p = jax.nn.softmax(sc, axis=-1).astype(vg.dtype)
    o = jnp.einsum('...qgrk,...qkgd->...qgrd', p, vg)
    return o.reshape(*lead, nq, N_HEADS_DSA * dh)


def prompt_mixers(q_a, k_a, v_a, q_b, k_b, v_b, q_i, k_i, w_i):
    bsz, seq = q_a.shape[:2]
    nb = seq // Q_BLOCK
    topk = min(TOPK_MAX, seq // 4)
    kpos = jnp.arange(seq)
    bidx = jnp.arange(bsz)[:, None, None]

    def to_blocks(t):
        return t.reshape(bsz, nb, Q_BLOCK, *t.shape[2:]).swapaxes(0, 1)

    def block(args):
        qa, qb, qi, wi, q0 = args
        qpos = q0 + jnp.arange(Q_BLOCK)
        z = jnp.einsum('bqhd,bshd->bhqs', qa, k_a).astype(jnp.float32) * SCALE
        a = stick_breaking_weights(z, kpos[None, :] < qpos[:, None])
        o_a = jnp.einsum('bhqs,bshd->bqhd', a.astype(v_a.dtype), v_a).reshape(bsz, Q_BLOCK, W_SB)
        scores = jnp.where(kpos[None, None, :] <= qpos[None, :, None], index_scores(qi, wi, k_i), -jnp.inf)
        _, idx = lax.top_k(scores, topk)
        valid = idx <= qpos[None, :, None]
        o_b = gathered_attention(qb, k_b[bidx, idx], v_b[bidx, idx], valid)
        return o_a, o_b

    q0s = jnp.arange(nb) * Q_BLOCK
    o_a, o_b = lax.map(block, (to_blocks(q_a), to_blocks(q_b), to_blocks(q_i), to_blocks(w_i), q0s))
    return o_a.swapaxes(0, 1).reshape(bsz, seq, W_SB), o_b.swapaxes(0, 1).reshape(bsz, seq, W_DSA_Q)


def sample_mixers(layer, q_a, k_a, v_a, q_b, k_b, v_b, q_i, k_i, w_i,
                  cache_sb_k, cache_sb_v, cache_dsa_k, cache_dsa_v, cache_idx_k, page_table):
    n_new = q_a.shape[1]
    past = page_table.shape[1] * PAGE_SIZE
    total = past + n_new
    topk = min(TOPK_MAX, total // 4)
    qpos = past + jnp.arange(n_new)
    kpos = jnp.arange(total)
    sb_mask = kpos[None, :] < qpos[:, None]
    idx_mask = kpos[None, :] <= qpos[:, None]

    def rows(pool, pt):
        return pool[layer, pt].reshape(past, *pool.shape[3:])

    def one_sequence(args):
        qa, ka, va, qb, kb, vb, qi, ki, wi, pt = args
        k_all = jnp.concatenate([rows(cache_sb_k, pt), ka], axis=0)
        v_all = jnp.concatenate([rows(cache_sb_v, pt), va], axis=0)
        z = jnp.einsum('qhd,shd->hqs', qa, k_all).astype(jnp.float32) * SCALE
        a = stick_breaking_weights(z, sb_mask)
        o_a = jnp.einsum('hqs,shd->qhd', a.astype(v_all.dtype), v_all).reshape(n_new, W_SB)
        ki_all = jnp.concatenate([rows(cache_idx_k, pt), ki], axis=0)
        scores = jnp.where(idx_mask, index_scores(qi, wi, ki_all), -jnp.inf)
        _, idx = lax.top_k(scores, topk)
        in_past = (idx < past)[..., None, None]
        pidx = jnp.minimum(idx, past - 1)
        page = pt[pidx // PAGE_SIZE]
        off = pidx % PAGE_SIZE
        nidx = jnp.clip(idx - past, 0, n_new - 1)
        kg = jnp.where(in_past, cache_dsa_k[layer, page, off], kb[nidx])
        vg = jnp.where(in_past, cache_dsa_v[layer, page, off], vb[nidx])
        o_b = gathered_attention(qb, kg, vg, idx <= qpos[:, None])
        return o_a, o_b

    return lax.map(one_sequence, (q_a, k_a, v_a, q_b, k_b, v_b, q_i, k_i, w_i, page_table))


def merge_branches(x, o_a, o_b, g_a, g_b, w_br_sb, w_br_dsa, w_out):
    mixed = jax.nn.sigmoid(g_a) * (o_a @ w_br_sb) + jax.nn.sigmoid(g_b) * (o_b @ w_br_dsa)
    return x + mixed @ w_out


def setup_inputs(seed: int = 0) -> dict:
    key = jax.random.key(seed)
    ks = jax.random.split(key, 24)
    n_pages = PAST_LEN // PAGE_SIZE
    n_used = DEC_BATCH * n_pages
    n_pool = n_used + n_used // POOL_EXTRA_DIV

    def normal(k, shape, scale=1.0):
        return jax.random.normal(k, shape, jnp.float32) * scale

    def gain(k, shape):
        return 1.0 + 0.02 * normal(k, shape)

    page_table = jax.random.permutation(ks[7], n_pool)[:n_used].reshape(DEC_BATCH, n_pages).astype(jnp.int32)
    return {
        'x_prompt': normal(ks[0], (BATCH, SEQ, D_MODEL)),
        'x_sample': normal(ks[1], (DEC_BATCH, DEC_SEQ, D_MODEL)),
        'cache_sb_k': normal(ks[2], (DEPTH, n_pool, PAGE_SIZE, N_HEADS_SB, HEAD_DIM)),
        'cache_sb_v': normal(ks[3], (DEPTH, n_pool, PAGE_SIZE, N_HEADS_SB, HEAD_DIM)),
        'cache_dsa_k': normal(ks[4], (DEPTH, n_pool, PAGE_SIZE, N_KV_DSA, HEAD_DIM)),
        'cache_dsa_v': normal(ks[5], (DEPTH, n_pool, PAGE_SIZE, N_KV_DSA, HEAD_DIM)),
        'cache_idx_k': normal(ks[6], (DEPTH, n_pool, PAGE_SIZE, IDX_DIM)),
        'page_table': page_table,
        'ffn1_g': gain(ks[8], (DEPTH, D_MODEL)),
        'ffn1_w1': normal(ks[9], (DEPTH, D_MODEL, D_FF), D_MODEL ** -0.5),
        'ffn1_w3': normal(ks[10], (DEPTH, D_MODEL, D_FF), D_MODEL ** -0.5),
        'ffn1_w2': normal(ks[11], (DEPTH, D_FF, D_MODEL), D_FF ** -0.5),
        'mix_g': gain(ks[12], (DEPTH, D_MODEL)),
        'w_in': normal(ks[13], (DEPTH, D_MODEL, D_IN), D_MODEL ** -0.5),
        'w_br_sb': normal(ks[14], (DEPTH, W_SB, D_MODEL), W_SB ** -0.5),
        'w_br_dsa': normal(ks[15], (DEPTH, W_DSA_Q, D_MODEL), W_DSA_Q ** -0.5),
        'w_out': normal(ks[16], (DEPTH, D_MODEL, D_MODEL), D_MODEL ** -0.5),
        'ffn2_g': gain(ks[17], (DEPTH, D_MODEL)),
        'ffn2_w1': normal(ks[18], (DEPTH, D_MODEL, D_FF), D_MODEL ** -0.5),
        'ffn2_w3': normal(ks[19], (DEPTH, D_MODEL, D_FF), D_MODEL ** -0.5),
        'ffn2_w2': normal(ks[20], (DEPTH, D_FF, D_MODEL), D_FF ** -0.5),
        'final_g': gain(ks[21], (D_MODEL,)),
    }


def reference(x_prompt, x_sample, cache_sb_k, cache_sb_v, cache_dsa_k, cache_dsa_v, cache_idx_k, page_table,
              ffn1_g, ffn1_w1, ffn1_w3, ffn1_w2, mix_g, w_in, w_br_sb, w_br_dsa, w_out,
              ffn2_g, ffn2_w1, ffn2_w3, ffn2_w2, final_g):
    pos_prompt = jnp.arange(x_prompt.shape[1])
    pos_sample = page_table.shape[1] * PAGE_SIZE + jnp.arange(x_sample.shape[1])
    xp, xs = x_prompt, x_sample
    st_p = [[] for _ in range(5)]
    st_s = [[] for _ in range(5)]
    for layer in range(DEPTH):
        xp = macaron_half(xp, ffn1_g[layer], ffn1_w1[layer], ffn1_w3[layer], ffn1_w2[layer])
        xs = macaron_half(xs, ffn1_g[layer], ffn1_w1[layer], ffn1_w3[layer], ffn1_w2[layer])
        qa, ka, va, qb, kb, vb, qi, ki, wi, ga, gb = mixer_inputs(rms_norm(xp, mix_g[layer]), pos_prompt, w_in[layer])
        oa, ob = prompt_mixers(qa, ka, va, qb, kb, vb, qi, ki, wi)
        xp = merge_branches(xp, oa, ob, ga, gb, w_br_sb[layer], w_br_dsa[layer], w_out[layer])
        for store, val in zip(st_p, (ka, va, kb, vb, ki)):
            store.append(val)
        qa, ka, va, qb, kb, vb, qi, ki, wi, ga, gb = mixer_inputs(rms_norm(xs, mix_g[layer]), pos_sample, w_in[layer])
        oa, ob = sample_mixers(layer, qa, ka, va, qb, kb, vb, qi, ki, wi,
                               cache_sb_k, cache_sb_v, cache_dsa_k, cache_dsa_v, cache_idx_k, page_table)
        xs = merge_branches(xs, oa, ob, ga, gb, w_br_sb[layer], w_br_dsa[layer], w_out[layer])
        for store, val in zip(st_s, (ka, va, kb, vb, ki)):
            store.append(val)
        xp = macaron_half(xp, ffn2_g[layer], ffn2_w1[layer], ffn2_w3[layer], ffn2_w2[layer])
        xs = macaron_half(xs, ffn2_g[layer], ffn2_w1[layer], ffn2_w3[layer], ffn2_w2[layer])
    y_prompt = rms_norm(xp, final_g)
    y_sample = rms_norm(xs, final_g)
    return (y_prompt, y_sample,
            jnp.stack(st_p[0]), jnp.stack(st_p[1]), jnp.stack(st_p[2]), jnp.stack(st_p[3]), jnp.stack(st_p[4]),
            jnp.stack(st_s[0]), jnp.stack(st_s[1]), jnp.stack(st_s[2]), jnp.stack(st_s[3]), jnp.stack(st_s[4]))
```

```python
import functools

import jax
import jax.numpy as jnp
import numpy as np
from jax import lax
from jax.experimental import pallas as pl
from jax.experimental.pallas import tpu as pltpu

HEAD_DIM = 128
IDX_DIM = 64
TOPK_MAX = 256
ROPE_THETA = 500000.0
ROT_FRACTION = 4
EPS = 1e-6
PAGE_SIZE = 128
SCALE = HEAD_DIM ** -0.5

LANES = 128
VMEM_LIMIT = 56 * 1024 * 1024

INT_MIN = -(2 ** 31)
NEG_KEY = -2139095041
SOFTMAX_M_INIT = -1e30
MASK_BIAS = -2e30

F32 = jnp.float32
BF16 = jnp.bfloat16
I32 = jnp.int32

_NT = (((1,), (1,)), ((), ()))


def _params(sem):
    return pltpu.CompilerParams(dimension_semantics=sem, vmem_limit_bytes=VMEM_LIMIT)


def _rms(x, g):
    return x * lax.rsqrt(jnp.mean(x * x, axis=-1, keepdims=True) + EPS) * g


def _log_sigmoid(z):
    return jnp.minimum(z, 0.0) - jnp.log(1.0 + jnp.exp(-jnp.abs(z)))


def _order_key(x):
    bits = pltpu.bitcast(x, I32)
    bits = jnp.where(bits == INT_MIN, 0, bits)
    return jnp.where(bits < 0, bits ^ 0x7FFFFFFF, bits)


def _split_bf16(x):
    hi = x.astype(BF16)
    lo = (x - hi.astype(F32)).astype(BF16)
    return hi, lo


def _rms_kernel(x_ref, g_ref, h_ref):
    h_ref[...] = _rms(x_ref[...], g_ref[...]).astype(BF16)


def _rms_call(x, g, tm):
    n, d = x.shape
    return pl.pallas_call(
        _rms_kernel,
        grid=(n // tm,),
        in_specs=[pl.BlockSpec((tm, d), lambda i: (i, 0)), pl.BlockSpec((1, d), lambda i: (0, 0))],
        out_specs=pl.BlockSpec((tm, d), lambda i: (i, 0)),
        out_shape=jax.ShapeDtypeStruct((n, d), BF16),
        compiler_params=_params(("parallel",)),
    )(x, g.reshape(1, d))


def _ffn_kernel(x_ref, h_ref, w1_ref, w3_ref, w2_ref, g_ref, *rest, final):
    if final:
        y_ref, acc_ref = rest
    else:
        y_ref, hn_ref, acc_ref = rest
    f = pl.program_id(1)

    @pl.when(f == 0)
    def _():
        acc_ref[...] = jnp.zeros_like(acc_ref)

    h = h_ref[...]
    a = jnp.dot(h, w1_ref[...], preferred_element_type=F32)
    b = jnp.dot(h, w3_ref[...], preferred_element_type=F32)
    u = (a * jax.nn.sigmoid(a)) * b
    acc_ref[...] += jnp.dot(u.astype(BF16), w2_ref[...], preferred_element_type=F32)

    @pl.when(f == pl.num_programs(1) - 1)
    def _():
        y = x_ref[...] + 0.5 * acc_ref[...]
        if final:
            y_ref[...] = _rms(y, g_ref[...])
        else:
            y_ref[...] = y
            hn_ref[...] = _rms(y, g_ref[...]).astype(BF16)


def _ffn_call(x, h, w1, w3, w2, g_next, *, final, tm, tf):
    n, d = x.shape
    dff = w1.shape[1]
    out_shape = [jax.ShapeDtypeStruct((n, d), F32)]
    out_specs = [pl.BlockSpec((tm, d), lambda i, f: (i, 0))]
    if not final:
        out_shape.append(jax.ShapeDtypeStruct((n, d), BF16))
        out_specs.append(pl.BlockSpec((tm, d), lambda i, f: (i, 0)))
    res = pl.pallas_call(
        functools.partial(_ffn_kernel, final=final),
        grid=(n // tm, dff // tf),
        in_specs=[
            pl.BlockSpec((tm, d), lambda i, f: (i, 0)),
            pl.BlockSpec((tm, d), lambda i, f: (i, 0)),
            pl.BlockSpec((d, tf), lambda i, f: (0, f)),
            pl.BlockSpec((d, tf), lambda i, f: (0, f)),
            pl.BlockSpec((tf, d), lambda i, f: (f, 0)),
            pl.BlockSpec((1, d), lambda i, f: (0, 0)),
        ],
        out_specs=out_specs,
        out_shape=out_shape,
        scratch_shapes=[pltpu.VMEM((tm, d), F32)],
        compiler_params=_params(("parallel", "arbitrary")),
    )(x, h, w1, w3, w2, g_next.reshape(1, d))
    return res[0] if final else res


def _rope(x, cos, sin_lo, sin_hi, half):
    return (x * cos + pltpu.roll(x, LANES - half, axis=1) * sin_lo
            + pltpu.roll(x, half, axis=1) * sin_hi)


def _inproj_kernel(h_ref, w_ref, cd_ref, sdl_ref, sdh_ref, ci_ref, sil_ref, sih_ref,
                   qa_ref, ka_ref, kab_ref, va_ref, vab_ref, qb_ref, qi_ref,
                   kb_ref, kbb_ref, vb_ref, vbb_ref, ki_ref, kib_ref, wi_ref, *, wkv):
    j = pl.program_id(1)
    acc = jnp.dot(h_ref[...], w_ref[0], preferred_element_type=F32)
    width = acc.shape[1]
    rot_d = HEAD_DIM // ROT_FRACTION // 2
    rot_i = IDX_DIM // ROT_FRACTION // 2

    def rope_cols(x, idx):
        if idx:
            c, sl, sh, half = ci_ref[...], sil_ref[...], sih_ref[...], rot_i
        else:
            c, sl, sh, half = cd_ref[...], sdl_ref[...], sdh_ref[...], rot_d
        return jnp.concatenate(
            [_rope(x[:, s:s + LANES], c, sl, sh, half) for s in range(0, x.shape[1], LANES)], axis=1)

    @pl.when(j == 0)
    def _():
        qa_ref[...] = acc.astype(BF16)

    @pl.when(j == 1)
    def _():
        ka_ref[...] = acc
        kab_ref[...] = acc.astype(BF16)

    @pl.when(j == 2)
    def _():
        va_ref[...] = acc
        vab_ref[...] = acc.astype(BF16)

    @pl.when(j == 3)
    def _():
        qb_ref[...] = rope_cols(acc, False).astype(BF16)

    @pl.when(j == 4)
    def _():
        qi_ref[...] = rope_cols(acc, True).astype(BF16)

    @pl.when(j == 5)
    def _():
        kb = rope_cols(acc[:, :wkv], False)
        kb_ref[...] = kb
        kbb_ref[...] = kb.astype(BF16)
        vb = acc[:, wkv:2 * wkv]
        vb_ref[...] = vb
        vbb_ref[...] = vb.astype(BF16)
        ki = rope_cols(acc[:, 2 * wkv:2 * wkv + LANES], True)[:, :IDX_DIM]
        ki_ref[...] = ki
        kib_ref[...] = ki.astype(BF16)
        wi_ref[...] = acc[:, 2 * wkv + LANES:2 * wkv + LANES + wi_ref.shape[1]]
    del width


def _inproj_call(h, w_tiles, tabs, *, tm, n_tab_blocks, wkv, n_idx_heads):
    n, d = h.shape
    nt, _, tw = w_tiles.shape
    row = lambda i, j: (i, 0)
    tab = lambda i, j: (i % n_tab_blocks, 0)
    f32o = lambda w: jax.ShapeDtypeStruct((n, w), F32)
    bf16o = lambda w: jax.ShapeDtypeStruct((n, w), BF16)
    widths = [(tw, BF16), (tw, F32), (tw, BF16), (tw, F32), (tw, BF16), (tw, BF16), (tw, BF16),
              (wkv, F32), (wkv, BF16), (wkv, F32), (wkv, BF16), (IDX_DIM, F32), (IDX_DIM, BF16),
              (n_idx_heads, F32)]
    del f32o, bf16o
    return pl.pallas_call(
        functools.partial(_inproj_kernel, wkv=wkv),
        grid=(n // tm, nt),
        in_specs=[pl.BlockSpec((tm, d), row), pl.BlockSpec((1, d, tw), lambda i, j: (j, 0, 0))]
        + [pl.BlockSpec((tm, LANES), tab)] * 6,
        out_specs=[pl.BlockSpec((tm, w), row) for w, _ in widths],
        out_shape=[jax.ShapeDtypeStruct((n, w), dt) for w, dt in widths],
        compiler_params=_params(("parallel", "arbitrary")),
    )(h, w_tiles, *tabs)


def _gates_kernel(h_ref, w_ref, o_ref):
    o_ref[...] = jax.nn.sigmoid(jnp.dot(h_ref[...], w_ref[...], preferred_element_type=F32))


def _gates_call(h, wg, *, tm, tn):
    n, d = h.shape
    ncol = wg.shape[1]
    return pl.pallas_call(
        _gates_kernel,
        grid=(n // tm, ncol // tn),
        in_specs=[pl.BlockSpec((tm, d), lambda i, j: (i, 0)), pl.BlockSpec((d, tn), lambda i, j: (0, j))],
        out_specs=pl.BlockSpec((tm, tn), lambda i, j: (i, j)),
        out_shape=jax.ShapeDtypeStruct((n, ncol), F32),
        compiler_params=_params(("parallel", "arbitrary")),
    )(h, wg)


def _sb_weights(z, tri, carry, mask):
    ls = _log_sigmoid(z)
    lr = ls - z
    if mask is not None:
        lr = jnp.where(mask, lr, 0.0)
    hi, lo = _split_bf16(lr)
    excl = (jnp.dot(hi, tri, preferred_element_type=F32)
            + jnp.dot(lo, tri, preferred_element_type=F32))
    a = jnp.exp(ls + excl + carry)
    if mask is not None:
        a = jnp.where(mask, a, 0.0)
    return a, carry + excl[:, 0:1] + lr[:, 0:1]


def _sb_prompt_kernel(q_ref, k_ref, v_ref, tri_ref, o_ref, *, tq):
    qi = pl.program_id(2)
    q = q_ref[...]
    tri = tri_ref[...]
    rows = lax.broadcasted_iota(I32, (tq, tq), 0)
    cols = lax.broadcasted_iota(I32, (tq, tq), 1)

    def block(kb, carry, acc, mask):
        ks = pl.multiple_of(kb * tq, tq)
        k = k_ref[pl.ds(ks, tq), :]
        v = v_ref[pl.ds(ks, tq), :]
        z = lax.dot_general(q, k, _NT, preferred_element_type=F32) * SCALE
        a, carry = _sb_weights(z, tri, carry, mask)
        return carry, acc + jnp.dot(a.astype(BF16), v, preferred_element_type=F32)

    carry, acc = block(qi, jnp.zeros((tq, 1), F32), jnp.zeros((tq, HEAD_DIM), F32), cols < rows)

    def body(it, state):
        return block(qi - 1 - it, state[0], state[1], None)

    _, acc = lax.fori_loop(0, qi, body, (carry, acc))
    o_ref[...] = acc.astype(BF16)


def _sb_prompt_call(qa, ka, va, tri, *, bsz, seq, n_heads, tq):
    q3 = qa.reshape(bsz, seq, n_heads * HEAD_DIM)
    k3 = ka.reshape(bsz, seq, n_heads * HEAD_DIM)
    v3 = va.reshape(bsz, seq, n_heads * HEAD_DIM)
    qspec = pl.BlockSpec((None, tq, HEAD_DIM), lambda b, h, i: (b, i, h))
    kspec = pl.BlockSpec((None, seq, HEAD_DIM), lambda b, h, i: (b, 0, h))
    out = pl.pallas_call(
        functools.partial(_sb_prompt_kernel, tq=tq),
        grid=(bsz, n_heads, seq // tq),
        in_specs=[qspec, kspec, kspec, pl.BlockSpec((tq, tq), lambda b, h, i: (0, 0))],
        out_specs=qspec,
        out_shape=jax.ShapeDtypeStruct((bsz, seq, n_heads * HEAD_DIM), BF16),
        compiler_params=_params(("parallel", "parallel", "arbitrary")),
    )(q3, k3, v3, tri)
    return out.reshape(bsz * seq, n_heads * HEAD_DIM)


def _kth_largest_key(count_ge, shape, topk):
    def body(it, pat):
        cand = pat ^ jnp.left_shift(jnp.int32(1), 31 - it)
        return jnp.where(count_ge(cand) >= topk, cand, pat)

    return lax.fori_loop(0, 32, body, jnp.full(shape, INT_MIN, I32))


def _tie_cut(count_eq_before, need, shape, nbits):
    def body(it, c):
        cand = c | jnp.left_shift(jnp.int32(1), nbits - 1 - it)
        return jnp.where(count_eq_before(cand) < need, cand, c)

    return lax.fori_loop(0, nbits, body, jnp.zeros(shape, I32)) + 1


def _dsa_prompt_kernel(qi_ref, wi_ref, ki_ref, qb_ref, kb_ref, vb_ref, o_ref,
                       key_ref, m_ref, l_ref, acc_ref, *, tq, tk, topk, n_idx_heads, n_kv, rep, pos_bits):
    i = pl.program_id(1)
    q0 = i * tq
    nkb = (q0 + tq + tk - 1) // tk
    rows = q0 + lax.broadcasted_iota(I32, (tq, tk), 0)
    cols = lax.broadcasted_iota(I32, (tq, tk), 1)
    qi = qi_ref[...]
    wi = wi_ref[...]

    def score_block(kb, _):
        ks = pl.multiple_of(kb * tk, tk)
        kblk = ki_ref[pl.ds(ks, tk), :]
        sc = jnp.zeros((tq, tk), F32)
        for h in range(n_idx_heads):
            logit = lax.dot_general(qi[:, h * IDX_DIM:(h + 1) * IDX_DIM], kblk, _NT,
                                    preferred_element_type=F32)
            sc = sc + jnp.maximum(logit, 0.0) * wi[:, h:h + 1]
        key_ref[:, pl.ds(ks, tk)] = jnp.where(ks + cols <= rows, _order_key(sc), NEG_KEY)
        return 0

    lax.fori_loop(0, nkb, score_block, 0)

    def count(pred):
        def body(kb, cnt):
            ks = pl.multiple_of(kb * tk, tk)
            hit = pred(key_ref[:, pl.ds(ks, tk)], ks + cols).astype(I32)
            part = hit[:, 0:LANES]
            for s in range(LANES, tk, LANES):
                part = part + hit[:, s:s + LANES]
            return cnt + part

        cnt = lax.fori_loop(0, nkb, body, jnp.zeros((tq, LANES), I32))
        return jnp.sum(cnt, axis=1, keepdims=True)

    count_ge = lambda v: count(lambda key, pos: key >= v)
    kth = _kth_largest_key(count_ge, (tq, 1), topk)
    thr = jnp.maximum(kth, NEG_KEY + 1)

    n_ge = count_ge(thr)

    @pl.when(jnp.max(n_ge) > topk)
    def _():
        need = topk - count(lambda key, pos: key > thr)
        cut = _tie_cut(lambda c: count(lambda key, pos: (key == thr) & (pos < c)), need, (tq, 1), pos_bits)

        def drop(kb, _):
            ks = pl.multiple_of(kb * tk, tk)
            key = key_ref[:, pl.ds(ks, tk)]
            key_ref[:, pl.ds(ks, tk)] = jnp.where((key == thr) & (ks + cols >= cut), NEG_KEY, key)
            return 0

        lax.fori_loop(0, nkb, drop, 0)

    m_ref[...] = jnp.full(m_ref.shape, SOFTMAX_M_INIT, F32)
    l_ref[...] = jnp.zeros_like(l_ref)
    acc_ref[...] = jnp.zeros_like(acc_ref)
    qb = qb_ref[...]
    qg = [jnp.concatenate([qb[:, (g * rep + r) * HEAD_DIM:(g * rep + r + 1) * HEAD_DIM] for r in range(rep)],
                          axis=0) for g in range(n_kv)]

    def attend(kb, _):
        ks = pl.multiple_of(kb * tk, tk)
        bias = jnp.where(key_ref[:, pl.ds(ks, tk)] >= thr, 0.0, MASK_BIAS)
        bias = jnp.concatenate([bias] * rep, axis=0)
        for g in range(n_kv):
            kg = kb_ref[pl.ds(ks, tk), g * HEAD_DIM:(g + 1) * HEAD_DIM]
            vg = vb_ref[pl.ds(ks, tk), g * HEAD_DIM:(g + 1) * HEAD_DIM]
            s = lax.dot_general(qg[g], kg, _NT, preferred_element_type=F32) * SCALE + bias
            m_old = m_ref[g]
            m_new = jnp.maximum(m_old, jnp.max(s, axis=1, keepdims=True))
            p = jnp.exp(s - m_new)
            alpha = jnp.exp(m_old - m_new)
            l_ref[g] = alpha * l_ref[g] + jnp.sum(p, axis=1, keepdims=True)
            acc_ref[g] = alpha * acc_ref[g] + jnp.dot(p.astype(BF16), vg, preferred_element_type=F32)
            m_ref[g] = m_new
        return 0

    lax.fori_loop(0, nkb, attend, 0)
    for g in range(n_kv):
        out = acc_ref[g] / l_ref[g]
        for r in range(rep):
            h = g * rep + r
            o_ref[:, h * HEAD_DIM:(h + 1) * HEAD_DIM] = out[r * tq:(r + 1) * tq].astype(BF16)


def _dsa_prompt_call(qi, wi, ki, qb, kb, vb, *, bsz, seq, n_idx_heads, n_heads, n_kv, topk, tq, tk):
    rep = n_heads // n_kv
    r3 = lambda t: t.reshape(bsz, seq, t.shape[-1])
    qspec = lambda w: pl.BlockSpec((None, tq, w), lambda b, i: (b, i, 0))
    kspec = lambda w: pl.BlockSpec((None, seq, w), lambda b, i: (b, 0, 0))
    wq, wb = n_idx_heads * IDX_DIM, n_heads * HEAD_DIM
    out = pl.pallas_call(
        functools.partial(_dsa_prompt_kernel, tq=tq, tk=tk, topk=topk, n_idx_heads=n_idx_heads,
                          n_kv=n_kv, rep=rep, pos_bits=max(1, int(seq - 1).bit_length())),
        grid=(bsz, seq // tq),
        in_specs=[qspec(wq), qspec(n_idx_heads), kspec(IDX_DIM), qspec(wb),
                  kspec(n_kv * HEAD_DIM), kspec(n_kv * HEAD_DIM)],
        out_specs=qspec(wb),
        out_shape=jax.ShapeDtypeStruct((bsz, seq, wb), BF16),
        scratch_shapes=[pltpu.VMEM((tq, seq), I32),
                        pltpu.VMEM((n_kv, rep * tq, 1), F32),
                        pltpu.VMEM((n_kv, rep * tq, 1), F32),
                        pltpu.VMEM((n_kv, rep * tq, HEAD_DIM), F32)],
        compiler_params=_params(("parallel", "arbitrary")),
    )(r3(qi), r3(wi), r3(ki), r3(qb), r3(kb), r3(vb))
    return out.reshape(bsz * seq, wb)


def _sb_sample_kernel(pt_ref, q_ref, tri_ref, *rest, n_heads, pages_per_step):
    del pt_ref
    k_refs = rest[:pages_per_step]
    v_refs = rest[pages_per_step:2 * pages_per_step]
    o_ref, carry_ref, acc_ref = rest[2 * pages_per_step:]
    j = pl.program_id(1)

    @pl.when(j == 0)
    def _():
        carry_ref[...] = jnp.zeros_like(carry_ref)
        acc_ref[...] = jnp.zeros_like(acc_ref)

    q = q_ref[...]
    tri = tri_ref[...]
    head = lax.broadcasted_iota(I32, (n_heads, PAGE_SIZE), 0)
    carry = carry_ref[...]
    acc = acc_ref[...]
    for k_ref, v_ref in zip(k_refs, v_refs):
        z = jnp.zeros((n_heads, PAGE_SIZE), F32)
        for h in range(n_heads):
            zh = lax.dot_general(q, k_ref[:, h, :].astype(BF16), _NT, preferred_element_type=F32)
            z = jnp.where(head == h, zh, z)
        a, carry = _sb_weights(z * SCALE, tri, carry, None)
        ab = a.astype(BF16)
        for h in range(n_heads):
            oh = jnp.dot(ab, v_ref[:, h, :].astype(BF16), preferred_element_type=F32)
            acc = acc + jnp.where(head == h, oh, 0.0)
    carry_ref[...] = carry
    acc_ref[...] = acc

    @pl.when(j == pl.num_programs(1) - 1)
    def _():
        o_ref[...] = acc.astype(BF16)


def _page_specs(block, n_steps_pages, pages_per_step, layer, reverse):
    specs = []
    for p in range(pages_per_step):
        if reverse:
            pick = lambda b, j, pt, p=p: pt[b, n_steps_pages - 1 - (j * pages_per_step + p)]
        else:
            pick = lambda b, j, pt, p=p: pt[b, j * pages_per_step + p]
        zeros = (0,) * (len(block) - 2)
        specs.append(pl.BlockSpec(block, lambda b, j, pt, pick=pick: (layer, pick(b, j, pt)) + zeros))
    return specs


def _sb_sample_call(page_table, qa, cache_k, cache_v, tri, *, layer, pages_per_step):
    nb, n_pages = page_table.shape
    n_heads = cache_k.shape[3]
    q3 = qa.reshape(nb, n_heads, HEAD_DIM)
    page_block = (None, None, PAGE_SIZE, n_heads, HEAD_DIM)
    kspecs = _page_specs(page_block, n_pages, pages_per_step, layer, True)
    row = pl.BlockSpec((None, n_heads, HEAD_DIM), lambda b, j, pt: (b, 0, 0))
    out = pl.pallas_call(
        functools.partial(_sb_sample_kernel, n_heads=n_heads, pages_per_step=pages_per_step),
        grid_spec=pltpu.PrefetchScalarGridSpec(
            num_scalar_prefetch=1,
            grid=(nb, n_pages // pages_per_step),
            in_specs=[row, pl.BlockSpec((PAGE_SIZE, PAGE_SIZE), lambda b, j, pt: (0, 0))] + kspecs + kspecs,
            out_specs=row,
            scratch_shapes=[pltpu.VMEM((n_heads, 1), F32), pltpu.VMEM((n_heads, HEAD_DIM), F32)],
        ),
        out_shape=jax.ShapeDtypeStruct((nb, n_heads, HEAD_DIM), BF16),
        compiler_params=_params(("parallel", "arbitrary")),
    )(page_table, q3, tri, *([cache_k] * pages_per_step), *([cache_v] * pages_per_step))
    return out.reshape(nb, n_heads * HEAD_DIM)


def _idx_sample_kernel(pt_ref, qi_ref, wi_ref, kin_ref, *rest, pages_per_step, topk, pos_bits):
    del pt_ref
    page_refs = rest[:pages_per_step]
    key_ref, meta_ref = rest[pages_per_step:]
    j = pl.program_id(1)
    qi = qi_ref[...]
    wi = wi_ref[...]
    for p, page_ref in enumerate(page_refs):
        logit = lax.dot_general(qi, page_ref[...].astype(BF16), _NT, preferred_element_type=F32)
        sc = jnp.sum(jnp.maximum(logit, 0.0) * wi, axis=0, keepdims=True)
        key_ref[pl.ds(j * pages_per_step + p, 1), :] = _order_key(sc)

    @pl.when(j == pl.num_programs(1) - 1)
    def _():
        n_pages = key_ref.shape[0]
        past = n_pages * PAGE_SIZE
        pos = (lax.broadcasted_iota(I32, (n_pages, PAGE_SIZE), 0) * PAGE_SIZE
               + lax.broadcasted_iota(I32, (n_pages, PAGE_SIZE), 1))
        own = jnp.sum(qi.astype(F32) * kin_ref[...].astype(F32), axis=1, keepdims=True)
        own_key = _order_key(jnp.sum(jnp.maximum(own, 0.0) * wi, axis=0, keepdims=True))

        def total(x):
            return jnp.sum(jnp.sum(x.astype(I32), axis=1, keepdims=True), axis=0, keepdims=True)

        def count(pred):
            return total(pred(key_ref[...], pos)) + pred(own_key, past).astype(I32)

        count_ge = lambda v: count(lambda key, where: key >= v)
        thr = _kth_largest_key(count_ge, (1, 1), topk)
        need = topk - count(lambda key, where: key > thr)
        cut = _tie_cut(lambda c: count(lambda key, where: (key == thr) & (where < c)), need, (1, 1), pos_bits)
        keys = key_ref[...]
        key_ref[...] = jnp.where((keys == thr) & (pos >= cut), NEG_KEY, keys)
        own_sel = (own_key > thr) | ((own_key == thr) & (past < cut))
        rowid = lax.broadcasted_iota(I32, meta_ref.shape, 0)
        meta_ref[...] = jnp.where(rowid == 0, thr, own_sel.astype(I32))


def _idx_sample_call(page_table, qi, wi, ki_new, cache_idx, *, layer, pages_per_step, topk):
    nb, n_pages = page_table.shape
    nh = wi.shape[1]
    row = lambda s: pl.BlockSpec((None,) + s, lambda b, j, pt: (b, 0, 0))
    pspecs = _page_specs((None, None, PAGE_SIZE, IDX_DIM), n_pages, pages_per_step, layer, False)
    return pl.pallas_call(
        functools.partial(_idx_sample_kernel, pages_per_step=pages_per_step, topk=topk,
                          pos_bits=int(n_pages * PAGE_SIZE).bit_length()),
        grid_spec=pltpu.PrefetchScalarGridSpec(
            num_scalar_prefetch=1,
            grid=(nb, n_pages // pages_per_step),
            in_specs=[row((nh, IDX_DIM)), row((nh, 1)), row((1, IDX_DIM))] + pspecs,
            out_specs=[row((n_pages, PAGE_SIZE)), row((8, LANES))],
        ),
        out_shape=[jax.ShapeDtypeStruct((nb, n_pages, PAGE_SIZE), I32),
                   jax.ShapeDtypeStruct((nb, 8, LANES), I32)],
        compiler_params=_params(("parallel", "arbitrary")),
    )(page_table, qi.reshape(nb, nh, IDX_DIM), wi.reshape(nb, nh, 1), ki_new.reshape(nb, 1, IDX_DIM),
      *([cache_idx] * pages_per_step))


def _dsa_sample_kernel(pt_ref, key_ref, meta_ref, q_ref, kn_ref, vn_ref, *rest, n_heads, n_kv, pages_per_step):
    del pt_ref
    k_refs = rest[:pages_per_step]
    v_refs = rest[pages_per_step:2 * pages_per_step]
    o_ref, m_ref, l_ref, acc_ref = rest[2 * pages_per_step:]
    j = pl.program_id(1)
    rep = n_heads // n_kv

    @pl.when(j == 0)
    def _():
        m_ref[...] = jnp.full(m_ref.shape, SOFTMAX_M_INIT, F32)
        l_ref[...] = jnp.zeros_like(l_ref)
        acc_ref[...] = jnp.zeros_like(acc_ref)

    q = q_ref[...]
    meta = meta_ref[...]
    thr = meta[0:1, :]
    group = lax.broadcasted_iota(I32, (n_heads, PAGE_SIZE), 0) // rep
    m, l, acc = m_ref[...], l_ref[...], acc_ref[...]

    def update(m, l, acc, s, pv):
        m_new = jnp.maximum(m, jnp.max(s, axis=1, keepdims=True))
        p = jnp.exp(s - m_new)
        alpha = jnp.exp(m - m_new)
        return m_new, alpha * l + jnp.sum(p, axis=1, keepdims=True), alpha * acc + pv(p)

    for p_i, (k_ref, v_ref) in enumerate(zip(k_refs, v_refs)):
        page = j * pages_per_step + p_i
        bias = jnp.where(key_ref[pl.ds(page, 1), :] >= thr, 0.0, MASK_BIAS)
        s = jnp.zeros((n_heads, PAGE_SIZE), F32)
        for g in range(n_kv):
            sg = lax.dot_general(q, k_ref[:, g, :].astype(BF16), _NT, preferred_element_type=F32)
            s = jnp.where(group == g, sg, s)

        def pv(p, v_ref=v_ref):
            pb = p.astype(BF16)
            out = jnp.zeros((n_heads, HEAD_DIM), F32)
            for g in range(n_kv):
                og = jnp.dot(pb, v_ref[:, g, :].astype(BF16), preferred_element_type=F32)
                out = jnp.where(group == g, og, out)
            return out

        m, l, acc = update(m, l, acc, s * SCALE + bias, pv)
    m_ref[...], l_ref[...], acc_ref[...] = m, l, acc

    @pl.when(j == pl.num_programs(1) - 1)
    def _():
        own_bias = jnp.where(meta[1:2, 0:1] > 0, 0.0, MASK_BIAS)
        s_own = jnp.sum(q.astype(F32) * kn_ref[...].astype(F32), axis=1, keepdims=True) * SCALE + own_bias
        vn = vn_ref[...].astype(F32)
        _, l_f, acc_f = update(m, l, acc, s_own, lambda p: p.astype(BF16).astype(F32) * vn)
        o_ref[...] = (acc_f / l_f).astype(BF16)


def _dsa_sample_call(page_table, keys, meta, qb, kb_new, vb_new, cache_k, cache_v, *, layer, pages_per_step):
    nb, n_pages = page_table.shape
    n_kv = cache_k.shape[3]
    n_heads = qb.shape[1] // HEAD_DIM
    rep = n_heads // n_kv
    row = lambda s: pl.BlockSpec((None,) + s, lambda b, j, pt: (b, 0, 0))
    own = lambda t: jnp.repeat(t.reshape(nb, n_kv, HEAD_DIM), rep, axis=1)
    page_block = (None, None, PAGE_SIZE, n_kv, HEAD_DIM)
    kspecs = _page_specs(page_block, n_pages, pages_per_step, layer, False)
    out = pl.pallas_call(
        functools.partial(_dsa_sample_kernel, n_heads=n_heads, n_kv=n_kv, pages_per_step=pages_per_step),
        grid_spec=pltpu.PrefetchScalarGridSpec(
            num_scalar_prefetch=1,
            grid=(nb, n_pages // pages_per_step),
            in_specs=[row((n_pages, PAGE_SIZE)), row((8, LANES)), row((n_heads, HEAD_DIM)),
                      row((n_heads, HEAD_DIM)), row((n_heads, HEAD_DIM))] + kspecs + kspecs,
            out_specs=row((n_heads, HEAD_DIM)),
            scratch_shapes=[pltpu.VMEM((n_heads, 1), F32), pltpu.VMEM((n_heads, 1), F32),
                            pltpu.VMEM((n_heads, HEAD_DIM), F32)],
        ),
        out_shape=jax.ShapeDtypeStruct((nb, n_heads, HEAD_DIM), BF16),
        compiler_params=_params(("parallel", "arbitrary")),
    )(page_table, keys, meta, qb.reshape(nb, n_heads, HEAD_DIM), own(kb_new), own(vb_new),
      *([cache_k] * pages_per_step), *([cache_v] * pages_per_step))
    return out.reshape(nb, n_heads * HEAD_DIM)


def _merge_kernel(x_ref, oa_ref, ob_ref, ga_ref, gb_ref, wa_ref, wb_ref, wo_ref, g_ref, y_ref, hn_ref):
    mixed = (ga_ref[...] * jnp.dot(oa_ref[...], wa_ref[...], preferred_element_type=F32)
             + gb_ref[...] * jnp.dot(ob_ref[...], wb_ref[...], preferred_element_type=F32))
    y = x_ref[...] + jnp.dot(mixed.astype(BF16), wo_ref[...], preferred_element_type=F32)
    y_ref[...] = y
    hn_ref[...] = _rms(y, g_ref[...]).astype(BF16)


def _merge_call(x, oa, ob, gates, wa, wb, wo, g_next, *, tm):
    n, d = x.shape
    row = lambda w: pl.BlockSpec((tm, w), lambda i: (i, 0))
    const = lambda s: pl.BlockSpec(s, lambda i: (0, 0), pipeline_mode=pl.Buffered(1))
    return pl.pallas_call(
        _merge_kernel,
        grid=(n // tm,),
        in_specs=[row(d), row(oa.shape[1]), row(ob.shape[1]),
                  pl.BlockSpec((tm, d), lambda i: (i, 0)), pl.BlockSpec((tm, d), lambda i: (i, 1)),
                  const(wa.shape), const(wb.shape), const(wo.shape), const((1, d))],
        out_specs=[row(d), row(d)],
        out_shape=[jax.ShapeDtypeStruct((n, d), F32), jax.ShapeDtypeStruct((n, d), BF16)],
        compiler_params=_params(("parallel",)),
    )(x, oa, ob, gates, gates, wa, wb, wo, g_next.reshape(1, d))


def _rope_tables(pos, head_dim):
    rot = head_dim // ROT_FRACTION
    half = rot // 2
    inv = 1.0 / (ROPE_THETA ** (jnp.arange(half, dtype=F32) * 2.0 / rot))
    ang = pos.astype(F32)[:, None] * inv[None, :]
    cos, sin = jnp.cos(ang), jnp.sin(ang)
    lane = np.arange(LANES) % head_dim
    first, second = lane < half, (lane >= half) & (lane < rot)
    freq = np.where(second, lane - half, np.where(first, lane, 0))
    cos_t = jnp.where(first | second, cos[:, freq], 1.0)
    sin_lo = jnp.where(first, -sin[:, freq], 0.0)
    sin_hi = jnp.where(second, sin[:, freq], 0.0)
    return cos_t, sin_lo, sin_hi


def _strict_lower(n):
    return (np.arange(n)[:, None] > np.arange(n)[None, :]).astype(np.float32)


def _row_tile(n, pref):
    return pref if n % pref == 0 else n


def kernel(x_prompt, x_sample, cache_sb_k, cache_sb_v, cache_dsa_k, cache_dsa_v, cache_idx_k, page_table, ffn1_g, ffn1_w1, ffn1_w3, ffn1_w2, mix_g, w_in, w_br_sb, w_br_dsa, w_out, ffn2_g, ffn2_w1, ffn2_w3, ffn2_w2, final_g):
    bsz, seq, d = x_prompt.shape
    nb, n_new, _ = x_sample.shape
    assert n_new == 1, "the sample path is written for one new token per sequence"
    depth = ffn1_g.shape[0]
    n_sb = cache_sb_k.shape[3]
    n_kv = cache_dsa_k.shape[3]
    w_sb = n_sb * HEAD_DIM
    w_dq = w_br_dsa.shape[1]
    n_dsa = w_dq // HEAD_DIM
    wkv = n_kv * HEAD_DIM
    n_idx = (w_in.shape[2] - 3 * w_sb - w_dq - 2 * wkv - IDX_DIM - 2 * d) // (IDX_DIM + 1)
    w_iq = n_idx * IDX_DIM
    assert w_sb == w_dq == w_iq and 2 * wkv + 2 * LANES <= w_sb
    n_pages = page_table.shape[1]
    past = n_pages * PAGE_SIZE
    topk_p = min(TOPK_MAX, seq // 4)
    topk_s = min(TOPK_MAX, (past + n_new) // 4)

    pos_p = jnp.arange(seq)
    pos_s = jnp.full((nb,), past, jnp.int32)
    tabs_p = _rope_tables(pos_p, HEAD_DIM) + _rope_tables(pos_p, IDX_DIM)
    tabs_s = _rope_tables(pos_s, HEAD_DIM) + _rope_tables(pos_s, IDX_DIM)
    tq_sb = _row_tile(seq, 256)
    tri_q = jnp.asarray(_strict_lower(tq_sb), BF16)
    tri_page = jnp.asarray(_strict_lower(PAGE_SIZE), BF16)

    xp = x_prompt.reshape(bsz * seq, d)
    xs = x_sample.reshape(nb, d)
    tm_p = _row_tile(bsz * seq, 512)
    tm_in = _row_tile(seq, 512)
    tm_merge = _row_tile(bsz * seq, 256)
    tf = _row_tile(ffn1_w1.shape[2], 512)
    st_p = [[] for _ in range(5)]
    st_s = [[] for _ in range(5)]

    hp = _rms_call(xp, ffn1_g[0], tm_p)
    hs = _rms_call(xs, ffn1_g[0], nb)
    for layer in range(depth):
        bf = lambda w: w[layer].astype(BF16)
        w1a, w3a, w2a = bf(ffn1_w1), bf(ffn1_w3), bf(ffn1_w2)
        w1b, w3b, w2b = bf(ffn2_w1), bf(ffn2_w3), bf(ffn2_w2)
        wl = w_in[layer]
        off = np.cumsum([0, w_sb, w_sb, w_sb, w_dq, wkv, wkv, w_iq, IDX_DIM, n_idx, d, d])
        col = lambda k: wl[:, off[k]:off[k + 1]]
        pad = lambda w: jnp.zeros((d, w), F32)
        misc = jnp.concatenate([col(4), col(5), col(7), pad(LANES - IDX_DIM), col(8), pad(LANES - n_idx),
                                pad(w_sb - 2 * wkv - 2 * LANES)], axis=1)
        w_tiles = jnp.stack([col(0), col(1), col(2), col(3), col(6), misc]).astype(BF16)
        w_gates = wl[:, off[9]:off[11]].astype(BF16)
        wa, wb, wo = bf(w_br_sb), bf(w_br_dsa), bf(w_out)

        xp, hp = _ffn_call(xp, hp, w1a, w3a, w2a, mix_g[layer], final=False, tm=tm_p, tf=tf)
        xs, hs = _ffn_call(xs, hs, w1a, w3a, w2a, mix_g[layer], final=False, tm=nb, tf=tf)

        (qa, ka, kab, va, vab, qb, qi, kb, kbb, vb, vbb, ki, kib, wi) = _inproj_call(
            hp, w_tiles, tabs_p, tm=tm_in, n_tab_blocks=seq // tm_in, wkv=wkv, n_idx_heads=n_idx)
        gates = _gates_call(hp, w_gates, tm=tm_p, tn=_row_tile(2 * d, 1024))
        oa = _sb_prompt_call(qa, kab, vab, tri_q, bsz=bsz, seq=seq, n_heads=n_sb, tq=tq_sb)
        ob = _dsa_prompt_call(qi, wi, kib, qb, kbb, vbb, bsz=bsz, seq=seq, n_idx_heads=n_idx, n_heads=n_dsa,
                              n_kv=n_kv, topk=topk_p, tq=_row_tile(seq, 128), tk=_row_tile(seq, 512))
        xp, hp = _merge_call(xp, oa, ob, gates, wa, wb, wo, ffn2_g[layer], tm=tm_merge)
        for store, val, shape in zip(st_p, (ka, va, kb, vb, ki),
                                     ((n_sb, HEAD_DIM), (n_sb, HEAD_DIM), (n_kv, HEAD_DIM), (n_kv, HEAD_DIM), (IDX_DIM,))):
            store.append(val.reshape((bsz, seq) + shape))

        (qa, ka, kab, va, vab, qb, qi, kb, kbb, vb, vbb, ki, kib, wi) = _inproj_call(
            hs, w_tiles, tabs_s, tm=nb, n_tab_blocks=1, wkv=wkv, n_idx_heads=n_idx)
        gates = _gates_call(hs, w_gates, tm=nb, tn=_row_tile(2 * d, 1024))
        oa = _sb_sample_call(page_table, qa, cache_sb_k, cache_sb_v, tri_page, layer=layer,
                             pages_per_step=4 if n_pages % 4 == 0 else 1)
        keys, meta = _idx_sample_call(page_table, qi, wi, kib, cache_idx_k, layer=layer,
                                      pages_per_step=16 if n_pages % 16 == 0 else 1, topk=topk_s)
        ob = _dsa_sample_call(page_table, keys, meta, qb, kbb, vbb, cache_dsa_k, cache_dsa_v, layer=layer,
                              pages_per_step=8 if n_pages % 8 == 0 else 1)
        xs, hs = _merge_call(xs, oa, ob, gates, wa, wb, wo, ffn2_g[layer], tm=nb)
        for store, val, shape in zip(st_s, (ka, va, kb, vb, ki),
                                     ((n_sb, HEAD_DIM), (n_sb, HEAD_DIM), (n_kv, HEAD_DIM), (n_kv, HEAD_DIM), (IDX_DIM,))):
            store.append(val.reshape((nb, n_new) + shape))

        last = layer == depth - 1
        g_next = final_g if last else ffn1_g[layer + 1]
        if last:
            xp = _ffn_call(xp, hp, w1b, w3b, w2b, g_next, final=True, tm=tm_p, tf=tf)
            xs = _ffn_call(xs, hs, w1b, w3b, w2b, g_next, final=True, tm=nb, tf=tf)
        else:
            xp, hp = _ffn_call(xp, hp, w1b, w3b, w2b, g_next, final=False, tm=tm_p, tf=tf)
            xs, hs = _ffn_call(xs, hs, w1b, w3b, w2b, g_next, final=False, tm=nb, tf=tf)

    return (xp.reshape(bsz, seq, d), xs.reshape(nb, n_new, d),
            jnp.stack(st_p[0]), jnp.stack(st_p[1]), jnp.stack(st_p[2]), jnp.stack(st_p[3]), jnp.stack(st_p[4]),
            jnp.stack(st_s[0]), jnp.stack(st_s[1]), jnp.stack(st_s[2]), jnp.stack(st_s[3]), jnp.stack(st_s[4]))
```

```python
import functools

import jax
import jax.numpy as jnp
import numpy as np
from jax import lax
from jax.experimental import pallas as pl
from jax.experimental.pallas import tpu as pltpu

HEAD_DIM = 128
IDX_DIM = 64
TOPK_MAX = 256
ROPE_THETA = 500000.0
ROT_FRACTION = 4
EPS = 1e-6
PAGE_SIZE = 128
SCALE = HEAD_DIM ** -0.5

LANES = 128
VMEM_LIMIT = 56 * 1024 * 1024

INT_MIN = -(2 ** 31)
NEG_KEY = -2139095041
SOFTMAX_M_INIT = -1e30
MASK_BIAS = -2e30
SB_LOG_UNDERFLOW = -110.0

F32 = jnp.float32
BF16 = jnp.bfloat16
I32 = jnp.int32

_NT = (((1,), (1,)), ((), ()))


def _params(sem):
    return pltpu.CompilerParams(dimension_semantics=sem, vmem_limit_bytes=VMEM_LIMIT)


def _rms(x, g):
    return x * lax.rsqrt(jnp.mean(x * x, axis=-1, keepdims=True) + EPS) * g


def _log_sigmoid(z):
    return jnp.minimum(z, 0.0) - jnp.log(1.0 + jnp.exp(-jnp.abs(z)))


def _order_key(x):
    bits = pltpu.bitcast(x, I32)
    bits = jnp.where(bits == INT_MIN, 0, bits)
    return jnp.where(bits < 0, bits ^ 0x7FFFFFFF, bits)


def _split_bf16(x):
    hi = x.astype(BF16)
    lo = (x - hi.astype(F32)).astype(BF16)
    return hi, lo


def _rms_kernel(x_ref, g_ref, h_ref):
    h_ref[...] = _rms(x_ref[...], g_ref[...]).astype(BF16)


def _rms_call(x, g, tm):
    n, d = x.shape
    return pl.pallas_call(
        _rms_kernel,
        grid=(n // tm,),
        in_specs=[pl.BlockSpec((tm, d), lambda i: (i, 0)), pl.BlockSpec((1, d), lambda i: (0, 0))],
        out_specs=pl.BlockSpec((tm, d), lambda i: (i, 0)),
        out_shape=jax.ShapeDtypeStruct((n, d), BF16),
        compiler_params=_params(("parallel",)),
    )(x, g.reshape(1, d))


def _ffn_kernel(x_ref, h_ref, w1_ref, w3_ref, w2_ref, g_ref, *rest, final):
    if final:
        y_ref, acc_ref = rest
    else:
        y_ref, hn_ref, acc_ref = rest
    f = pl.program_id(1)

    @pl.when(f == 0)
    def _():
        acc_ref[...] = jnp.zeros_like(acc_ref)

    h = h_ref[...]
    a = jnp.dot(h, w1_ref[...], preferred_element_type=F32)
    b = jnp.dot(h, w3_ref[...], preferred_element_type=F32)
    u = (a * jax.nn.sigmoid(a)) * b
    acc_ref[...] += jnp.dot(u.astype(BF16), w2_ref[...], preferred_element_type=F32)

    @pl.when(f == pl.num_programs(1) - 1)
    def _():
        y = x_ref[...] + 0.5 * acc_ref[...]
        if final:
            y_ref[...] = _rms(y, g_ref[...])
        else:
            y_ref[...] = y
            hn_ref[...] = _rms(y, g_ref[...]).astype(BF16)


def _ffn_call(x, h, w1, w3, w2, g_next, *, final, tm, tf):
    n, d = x.shape
    dff = w1.shape[1]
    out_shape = [jax.ShapeDtypeStruct((n, d), F32)]
    out_specs = [pl.BlockSpec((tm, d), lambda i, f: (i, 0))]
    if not final:
        out_shape.append(jax.ShapeDtypeStruct((n, d), BF16))
        out_specs.append(pl.BlockSpec((tm, d), lambda i, f: (i, 0)))
    res = pl.pallas_call(
        functools.partial(_ffn_kernel, final=final),
        grid=(n // tm, dff // tf),
        in_specs=[
            pl.BlockSpec((tm, d), lambda i, f: (i, 0)),
            pl.BlockSpec((tm, d), lambda i, f: (i, 0)),
            pl.BlockSpec((d, tf), lambda i, f: (0, f)),
            pl.BlockSpec((d, tf), lambda i, f: (0, f)),
            pl.BlockSpec((tf, d), lambda i, f: (f, 0)),
            pl.BlockSpec((1, d), lambda i, f: (0, 0)),
        ],
        out_specs=out_specs,
        out_shape=out_shape,
        scratch_shapes=[pltpu.VMEM((tm, d), F32)],
        compiler_params=_params(("parallel", "arbitrary")),
    )(x, h, w1, w3, w2, g_next.reshape(1, d))
    return res[0] if final else res


def _rope(x, cos, sin_lo, sin_hi, half):
    return (x * cos + pltpu.roll(x, LANES - half, axis=1) * sin_lo
            + pltpu.roll(x, half, axis=1) * sin_hi)


def _inproj_kernel(h_ref, w_ref, cd_ref, sdl_ref, sdh_ref, ci_ref, sil_ref, sih_ref,
                   qa_ref, ka_ref, kab_ref, va_ref, vab_ref, qb_ref, qi_ref,
                   kb_ref, kbb_ref, vb_ref, vbb_ref, ki_ref, kib_ref, wi_ref, *, wkv, query_on_lanes):
    j = pl.program_id(1)
    acc = jnp.dot(h_ref[...], w_ref[0], preferred_element_type=F32)
    lay = (lambda x: x.T) if query_on_lanes else (lambda x: x)
    rot_d = HEAD_DIM // ROT_FRACTION // 2
    rot_i = IDX_DIM // ROT_FRACTION // 2

    def rope_cols(x, idx):
        if idx:
            c, sl, sh, half = ci_ref[...], sil_ref[...], sih_ref[...], rot_i
        else:
            c, sl, sh, half = cd_ref[...], sdl_ref[...], sdh_ref[...], rot_d
        return jnp.concatenate(
            [_rope(x[:, s:s + LANES], c, sl, sh, half) for s in range(0, x.shape[1], LANES)], axis=1)

    @pl.when(j == 0)
    def _():
        qa_ref[...] = acc.astype(BF16)

    @pl.when(j == 1)
    def _():
        ka_ref[...] = acc
        kab_ref[...] = acc.astype(BF16)

    @pl.when(j == 2)
    def _():
        va_ref[...] = acc
        vab_ref[...] = acc.astype(BF16)

    @pl.when(j == 3)
    def _():
        qb_ref[...] = lay(rope_cols(acc, False)).astype(BF16)

    @pl.when(j == 4)
    def _():
        qi_ref[...] = lay(rope_cols(acc, True)).astype(BF16)

    @pl.when(j == 5)
    def _():
        kb = rope_cols(acc[:, :wkv], False)
        kb_ref[...] = kb
        kbb_ref[...] = kb.astype(BF16)
        vb = acc[:, wkv:2 * wkv]
        vb_ref[...] = vb
        ki = rope_cols(acc[:, 2 * wkv:2 * wkv + LANES], True)[:, :IDX_DIM]
        ki_ref[...] = ki
        kib_ref[...] = ki.astype(BF16)
        wi_lanes = acc[:, 2 * wkv + LANES:2 * wkv + 2 * LANES]
        if query_on_lanes:
            vbb_ref[0] = vb.T.astype(BF16)
            wi_ref[...] = wi_lanes.T[:wi_ref.shape[0], :]
        else:
            vbb_ref[...] = vb.astype(BF16)
            wi_ref[...] = wi_lanes[:, :wi_ref.shape[1]]


def _inproj_call(h, w_tiles, tabs, *, tm, n_tab_blocks, wkv, n_idx_heads, query_on_lanes):
    n, d = h.shape
    nt, _, tw = w_tiles.shape
    row = lambda i, j: (i, 0)
    tab = lambda i, j: (i % n_tab_blocks, 0)
    rows_major = lambda w, dt: (jax.ShapeDtypeStruct((n, w), dt), pl.BlockSpec((tm, w), row))
    if query_on_lanes:
        lanes_major = lambda w, dt: (jax.ShapeDtypeStruct((w, n), dt), pl.BlockSpec((w, tm), lambda i, j: (0, i)))
        v_blocks = (jax.ShapeDtypeStruct((n // tm, wkv, tm), BF16), pl.BlockSpec((1, wkv, tm), lambda i, j: (i, 0, 0)))
    else:
        lanes_major = rows_major
        v_blocks = rows_major(wkv, BF16)
    outs = [rows_major(tw, BF16), rows_major(tw, F32), rows_major(tw, BF16), rows_major(tw, F32),
            rows_major(tw, BF16), lanes_major(tw, BF16), lanes_major(tw, BF16),
            rows_major(wkv, F32), rows_major(wkv, BF16), rows_major(wkv, F32), v_blocks,
            rows_major(IDX_DIM, F32), rows_major(IDX_DIM, BF16), lanes_major(n_idx_heads, F32)]
    return pl.pallas_call(
        functools.partial(_inproj_kernel, wkv=wkv, query_on_lanes=query_on_lanes),
        grid=(n // tm, nt),
        in_specs=[pl.BlockSpec((tm, d), row), pl.BlockSpec((1, d, tw), lambda i, j: (j, 0, 0))]
        + [pl.BlockSpec((tm, LANES), tab)] * 6,
        out_specs=[spec for _, spec in outs],
        out_shape=[shape for shape, _ in outs],
        compiler_params=_params(("parallel", "arbitrary")),
    )(h, w_tiles, *tabs)


def _gates_kernel(h_ref, w_ref, o_ref):
    o_ref[...] = jax.nn.sigmoid(jnp.dot(h_ref[...], w_ref[...], preferred_element_type=F32))


def _gates_call(h, wg, *, tm, tn):
    n, d = h.shape
    ncol = wg.shape[1]
    return pl.pallas_call(
        _gates_kernel,
        grid=(n // tm, ncol // tn),
        in_specs=[pl.BlockSpec((tm, d), lambda i, j: (i, 0)), pl.BlockSpec((d, tn), lambda i, j: (0, j))],
        out_specs=pl.BlockSpec((tm, tn), lambda i, j: (i, j)),
        out_shape=jax.ShapeDtypeStruct((n, ncol), F32),
        compiler_params=_params(("parallel", "arbitrary")),
    )(h, wg)


def _sb_weights(z, sum_rhs, carry, mask):
    n = z.shape[1]
    ls = _log_sigmoid(z)
    lr = ls - z
    if mask is not None:
        lr = jnp.where(mask, lr, 0.0)
    hi, lo = _split_bf16(lr)
    sums = (jnp.dot(hi, sum_rhs, preferred_element_type=F32)
            + jnp.dot(lo, sum_rhs, preferred_element_type=F32))
    a = jnp.exp(ls + sums[:, :n] + carry)
    if mask is not None:
        a = jnp.where(mask, a, 0.0)
    return a, carry + sums[:, n:]


def _sb_prompt_kernel(q_ref, k_ref, v_ref, rhs_ref, o_ref, carry_ref, acc_ref, *, tq, n_heads):
    qi = pl.program_id(1)
    rhs = rhs_ref[...]
    rows = lax.broadcasted_iota(I32, (n_heads * tq, tq), 0) % tq
    cols = lax.broadcasted_iota(I32, (n_heads * tq, tq), 1)
    head_lanes = [slice(h * HEAD_DIM, (h + 1) * HEAD_DIM) for h in range(n_heads)]

    def block(kb, mask, first):
        ks = pl.multiple_of(kb * tq, tq)
        z = jnp.concatenate(
            [lax.dot_general(q_ref[:, hl], k_ref[pl.ds(ks, tq), hl], _NT, preferred_element_type=F32)
             for hl in head_lanes], axis=0) * SCALE
        a, carry = _sb_weights(z, rhs, 0.0 if first else carry_ref[...], mask)
        carry_ref[...] = carry
        ab = a.astype(BF16)
        pv = jnp.concatenate(
            [jnp.dot(ab[h * tq:(h + 1) * tq], v_ref[pl.ds(ks, tq), hl], preferred_element_type=F32)
             for h, hl in enumerate(head_lanes)], axis=0)
        acc_ref[...] = pv if first else acc_ref[...] + pv
        return jnp.max(carry)

    def more(state):
        return (state[0] >= 0) & (state[1] > SB_LOG_UNDERFLOW)

    def step(state):
        return state[0] - 1, block(state[0], None, False)

    lax.while_loop(more, step, (qi - 1, block(qi, cols < rows, True)))
    for h, hl in enumerate(head_lanes):
        o_ref[:, hl] = acc_ref[h * tq:(h + 1) * tq, :].astype(BF16)


def _sb_prompt_call(qa, ka, va, sum_rhs, *, bsz, seq, n_heads, tq):
    width = n_heads * HEAD_DIM
    q3 = qa.reshape(bsz, seq, width)
    k3 = ka.reshape(bsz, seq, width)
    v3 = va.reshape(bsz, seq, width)
    qspec = pl.BlockSpec((None, tq, width), lambda b, i: (b, i, 0))
    kspec = pl.BlockSpec((None, seq, width), lambda b, i: (b, 0, 0), pipeline_mode=pl.Buffered(1))
    out = pl.pallas_call(
        functools.partial(_sb_prompt_kernel, tq=tq, n_heads=n_heads),
        grid=(bsz, seq // tq),
        in_specs=[qspec, kspec, kspec, pl.BlockSpec((tq, 2 * tq), lambda b, i: (0, 0))],
        out_specs=qspec,
        out_shape=jax.ShapeDtypeStruct((bsz, seq, width), BF16),
        scratch_shapes=[pltpu.VMEM((n_heads * tq, tq), F32), pltpu.VMEM((n_heads * tq, HEAD_DIM), F32)],
        compiler_params=_params(("parallel", "arbitrary")),
    )(q3, k3, v3, sum_rhs)
    return out.reshape(bsz * seq, width)


def _kth_largest_key(count_ge, shape, topk):
    def body(it, pat):
        cand = pat ^ jnp.left_shift(jnp.int32(1), 31 - it)
        return jnp.where(count_ge(cand) >= topk, cand, pat)

    return lax.fori_loop(0, 32, body, jnp.full(shape, INT_MIN, I32))


def _tie_cut(count_eq_before, need, shape, nbits):
    def body(it, c):
        cand = c | jnp.left_shift(jnp.int32(1), nbits - 1 - it)
        return jnp.where(count_eq_before(cand) < need, cand, c)

    return lax.fori_loop(0, nbits, body, jnp.zeros(shape, I32)) + 1


def _dsa_prompt_kernel(qi_ref, wi_ref, ki_ref, qb_ref, kb_ref, vb_ref, o_ref,
                       key_ref, m_ref, l_ref, acc_ref, *, tq, tk, topk, n_idx_heads, n_kv, rep, pos_bits):
    i = pl.program_id(1)
    q0 = i * tq
    nkb = (q0 + tq + tk - 1) // tk
    kpos = lax.broadcasted_iota(I32, (tk, tq), 0)
    qpos = q0 + lax.broadcasted_iota(I32, (tk, tq), 1)

    pair_rhs = [jnp.concatenate([qi_ref[h * IDX_DIM:(h + 1) * IDX_DIM, :],
                                 qi_ref[(h + 1) * IDX_DIM:(h + 2) * IDX_DIM, :]], axis=1)
                for h in range(0, n_idx_heads, 2)]
    w_row = [wi_ref[h:h + 1, :] for h in range(n_idx_heads)]

    def score_block(kb, _):
        ks = pl.multiple_of(kb * tk, tk)
        kblk = ki_ref[pl.ds(ks, tk), :]
        sc = jnp.zeros((tk, tq), F32)
        for p, rhs in enumerate(pair_rhs):
            logit = jnp.dot(kblk, rhs, preferred_element_type=F32)
            sc = (sc + jnp.maximum(logit[:, :tq], 0.0) * w_row[2 * p]
                  + jnp.maximum(logit[:, tq:], 0.0) * w_row[2 * p + 1])
        key_ref[pl.ds(ks, tk), :] = jnp.where(ks + kpos <= qpos, _order_key(sc), NEG_KEY)
        return 0

    lax.fori_loop(0, nkb, score_block, 0)

    def count(pred):
        def body(kb, cnt):
            ks = pl.multiple_of(kb * tk, tk)
            hit = pred(key_ref[pl.ds(ks, tk), :], ks + kpos).astype(I32)
            return cnt + jnp.sum(hit.reshape(tk // 8, 8, tq), axis=0)

        cnt = lax.fori_loop(0, nkb, body, jnp.zeros((8, tq), I32))
        return jnp.sum(cnt, axis=0, keepdims=True)

    count_ge = lambda v: count(lambda key, pos: key >= v)
    kth = _kth_largest_key(count_ge, (1, tq), topk)
    thr = jnp.maximum(kth, NEG_KEY + 1)

    n_ge = count_ge(thr)

    @pl.when(jnp.max(n_ge) > topk)
    def _():
        need = topk - count(lambda key, pos: key > thr)
        cut = _tie_cut(lambda c: count(lambda key, pos: (key == thr) & (pos < c)), need, (1, tq), pos_bits)

        def drop(kb, _):
            ks = pl.multiple_of(kb * tk, tk)
            key = key_ref[pl.ds(ks, tk), :]
            key_ref[pl.ds(ks, tk), :] = jnp.where((key == thr) & (ks + kpos >= cut), NEG_KEY, key)
            return 0

        lax.fori_loop(0, nkb, drop, 0)

    m_ref[...] = jnp.full(m_ref.shape, SOFTMAX_M_INIT, F32)
    l_ref[...] = jnp.zeros_like(l_ref)
    acc_ref[...] = jnp.zeros_like(acc_ref)
    q_cols = [jnp.concatenate([qb_ref[(g * rep + r) * HEAD_DIM:(g * rep + r + 1) * HEAD_DIM, :]
                               for r in range(rep)], axis=1) for g in range(n_kv)]

    def attend(kb, _):
        ks = pl.multiple_of(kb * tk, tk)
        bias = jnp.where(key_ref[pl.ds(ks, tk), :] >= thr, 0.0, MASK_BIAS)
        bias = jnp.concatenate([bias] * rep, axis=1)
        for g in range(n_kv):
            kg = kb_ref[pl.ds(ks, tk), g * HEAD_DIM:(g + 1) * HEAD_DIM]
            s = jnp.dot(kg, q_cols[g], preferred_element_type=F32) * SCALE + bias
            m_old = m_ref[g]
            m_new = jnp.maximum(m_old, jnp.max(s, axis=0, keepdims=True))
            p = jnp.exp(s - m_new)
            alpha = jnp.exp(m_old - m_new)
            l_ref[g] = alpha * l_ref[g] + jnp.sum(p, axis=0, keepdims=True)
            v_t = vb_ref[kb, g * HEAD_DIM:(g + 1) * HEAD_DIM, :]
            acc_ref[g] = alpha * acc_ref[g] + jnp.dot(v_t, p.astype(BF16), preferred_element_type=F32)
            m_ref[g] = m_new
        return 0

    lax.fori_loop(0, nkb, attend, 0)
    for g in range(n_kv):
        out_t = acc_ref[g] / l_ref[g]
        for r in range(rep):
            h = g * rep + r
            o_ref[:, h * HEAD_DIM:(h + 1) * HEAD_DIM] = out_t[:, r * tq:(r + 1) * tq].T.astype(BF16)


def _dsa_prompt_call(qi_t, wi_t, ki, qb_t, kb, vb_t, *, bsz, seq, n_idx_heads, n_heads, n_kv, topk, tq):
    rep = n_heads // n_kv
    nq = seq // tq
    tk = vb_t.shape[2]
    wkv = n_kv * HEAD_DIM
    wq, wb = n_idx_heads * IDX_DIM, n_heads * HEAD_DIM
    qspec = lambda w: pl.BlockSpec((w, tq), lambda b, i: (0, b * nq + i))
    kspec = lambda w: pl.BlockSpec((None, seq, w), lambda b, i: (b, 0, 0))
    out = pl.pallas_call(
        functools.partial(_dsa_prompt_kernel, tq=tq, tk=tk, topk=topk, n_idx_heads=n_idx_heads,
                          n_kv=n_kv, rep=rep, pos_bits=max(1, int(seq - 1).bit_length())),
        grid=(bsz, nq),
        in_specs=[qspec(wq), qspec(n_idx_heads), kspec(IDX_DIM), qspec(wb), kspec(wkv),
                  pl.BlockSpec((seq // tk, wkv, tk), lambda b, i: (b, 0, 0))],
        out_specs=pl.BlockSpec((None, tq, wb), lambda b, i: (b, i, 0)),
        out_shape=jax.ShapeDtypeStruct((bsz, seq, wb), BF16),
        scratch_shapes=[pltpu.VMEM((seq, tq), I32),
                        pltpu.VMEM((n_kv, 1, rep * tq), F32),
                        pltpu.VMEM((n_kv, 1, rep * tq), F32),
                        pltpu.VMEM((n_kv, HEAD_DIM, rep * tq), F32)],
        compiler_params=_params(("parallel", "arbitrary")),
    )(qi_t, wi_t, ki.reshape(bsz, seq, IDX_DIM), qb_t, kb.reshape(bsz, seq, wkv), vb_t)
    return out.reshape(bsz * seq, wb)


def _sb_sample_kernel(pt_ref, q_ref, rhs_ref, ck_ref, cv_ref, o_ref, kbuf, vbuf, sem, *, layer, n_heads, n_pages):
    b = pl.program_id(0)
    q = q_ref[...]
    rhs = rhs_ref[...]
    head = lax.broadcasted_iota(I32, (n_heads, PAGE_SIZE), 0)

    def page_copies(n, slot):
        page = pt_ref[b, n_pages - 1 - n]
        return (pltpu.make_async_copy(ck_ref.at[layer, page], kbuf.at[slot], sem.at[0, slot]),
                pltpu.make_async_copy(cv_ref.at[layer, page], vbuf.at[slot], sem.at[1, slot]))

    def start(n, slot):
        for copy in page_copies(n, slot):
            copy.start()

    def wait(n, slot):
        for copy in page_copies(n, slot):
            copy.wait()

    def more(state):
        return (state[0] < n_pages) & (state[1] > SB_LOG_UNDERFLOW)

    def step(state):
        n, _, carry, acc = state
        slot = n % 2
        wait(n, slot)

        @pl.when(n + 1 < n_pages)
        def _():
            start(n + 1, 1 - slot)

        z = jnp.zeros((n_heads, PAGE_SIZE), F32)
        for h in range(n_heads):
            kh = kbuf[slot, pl.ds(h, PAGE_SIZE, stride=n_heads), :].astype(BF16)
            z = jnp.where(head == h, lax.dot_general(q, kh, _NT, preferred_element_type=F32), z)
        a, carry = _sb_weights(z * SCALE, rhs, carry, None)
        ab = a.astype(BF16)
        for h in range(n_heads):
            vh = vbuf[slot, pl.ds(h, PAGE_SIZE, stride=n_heads), :].astype(BF16)
            acc = acc + jnp.where(head == h, jnp.dot(ab, vh, preferred_element_type=F32), 0.0)
        return n + 1, jnp.max(carry), carry, acc

    start(0, 0)
    n_done, _, _, acc = lax.while_loop(
        more, step, (jnp.int32(0), jnp.float32(0.0), jnp.zeros((n_heads, PAGE_SIZE), F32),
                     jnp.zeros((n_heads, HEAD_DIM), F32)))

    @pl.when(n_done < n_pages)
    def _():
        wait(n_done, n_done % 2)

    o_ref[...] = acc.astype(BF16)


def _page_specs(block, n_steps_pages, pages_per_step, layer, reverse):
    specs = []
    for p in range(pages_per_step):
        if reverse:
            pick = lambda b, j, pt, p=p: pt[b, n_steps_pages - 1 - (j * pages_per_step + p)]
        else:
            pick = lambda b, j, pt, p=p: pt[b, j * pages_per_step + p]
        zeros = (0,) * (len(block) - 2)
        specs.append(pl.BlockSpec(block, lambda b, j, pt, pick=pick: (layer, pick(b, j, pt)) + zeros))
    return specs


def _flat_pages(cache):
    depth, pool, page, heads, dim = cache.shape
    return cache.reshape(depth, pool, page * heads, dim)


def _sb_sample_call(page_table, qa, cache_k, cache_v, sum_rhs, *, layer):
    nb, n_pages = page_table.shape
    n_heads = cache_k.shape[3]
    q3 = qa.reshape(nb, n_heads, HEAD_DIM)
    row = pl.BlockSpec((None, n_heads, HEAD_DIM), lambda b, pt: (b, 0, 0))
    page_rows = PAGE_SIZE * n_heads
    out = pl.pallas_call(
        functools.partial(_sb_sample_kernel, layer=layer, n_heads=n_heads, n_pages=n_pages),
        grid_spec=pltpu.PrefetchScalarGridSpec(
            num_scalar_prefetch=1,
            grid=(nb,),
            in_specs=[row, pl.BlockSpec((PAGE_SIZE, 2 * PAGE_SIZE), lambda b, pt: (0, 0)),
                      pl.BlockSpec(memory_space=pl.ANY), pl.BlockSpec(memory_space=pl.ANY)],
            out_specs=row,
            scratch_shapes=[pltpu.VMEM((2, page_rows, HEAD_DIM), F32), pltpu.VMEM((2, page_rows, HEAD_DIM), F32),
                            pltpu.SemaphoreType.DMA((2, 2))],
        ),
        out_shape=jax.ShapeDtypeStruct((nb, n_heads, HEAD_DIM), BF16),
        compiler_params=_params(("arbitrary",)),
    )(page_table, q3, sum_rhs, _flat_pages(cache_k), _flat_pages(cache_v))
    return out.reshape(nb, n_heads * HEAD_DIM)


def _idx_sample_kernel(pt_ref, qi_ref, wi_ref, kin_ref, *rest, pages_per_step, topk, pos_bits):
    del pt_ref
    page_refs = rest[:pages_per_step]
    key_ref, meta_ref = rest[pages_per_step:]
    j = pl.program_id(1)
    qi = qi_ref[...]
    wi = wi_ref[...]
    logit = jnp.concatenate(
        [jnp.dot(qi, page_ref[...].astype(BF16), preferred_element_type=F32)
         for page_ref in page_refs], axis=1)
    keys = _order_key(jnp.sum(jnp.maximum(logit, 0.0) * wi, axis=0, keepdims=True))
    for p in range(pages_per_step):
        key_ref[pl.ds(j * pages_per_step + p, 1), :] = keys[:, p * PAGE_SIZE:(p + 1) * PAGE_SIZE]

    @pl.when(j == pl.num_programs(1) - 1)
    def _():
        n_pages = key_ref.shape[0]
        past = n_pages * PAGE_SIZE
        pos = (lax.broadcasted_iota(I32, (n_pages, PAGE_SIZE), 0) * PAGE_SIZE
               + lax.broadcasted_iota(I32, (n_pages, PAGE_SIZE), 1))
        own = jnp.sum(qi.astype(F32) * kin_ref[...].astype(F32), axis=1, keepdims=True)
        own_key = _order_key(jnp.sum(jnp.maximum(own, 0.0) * wi, axis=0, keepdims=True))

        def total(x):
            return jnp.sum(jnp.sum(x.astype(I32), axis=1, keepdims=True), axis=0, keepdims=True)

        def count(pred):
            return total(pred(key_ref[...], pos)) + pred(own_key, past).astype(I32)

        count_ge = lambda v: count(lambda key, where: key >= v)
        thr = _kth_largest_key(count_ge, (1, 1), topk)
        need = topk - count(lambda key, where: key > thr)
        cut = _tie_cut(lambda c: count(lambda key, where: (key == thr) & (where < c)), need, (1, 1), pos_bits)
        keys = key_ref[...]
        key_ref[...] = jnp.where((keys == thr) & (pos >= cut), NEG_KEY, keys)
        own_sel = (own_key > thr) | ((own_key == thr) & (past < cut))
        rowid = lax.broadcasted_iota(I32, meta_ref.shape, 0)
        meta_ref[...] = jnp.where(rowid == 0, thr, own_sel.astype(I32))


def _idx_sample_call(page_table, qi, wi, ki_new, cache_idx, *, layer, pages_per_step, topk):
    nb, n_pages = page_table.shape
    nh = wi.shape[1]
    row = lambda s: pl.BlockSpec((None,) + s, lambda b, j, pt: (b, 0, 0))
    cache_t = jnp.swapaxes(cache_idx, 2, 3)
    pspecs = _page_specs((None, None, IDX_DIM, PAGE_SIZE), n_pages, pages_per_step, layer, False)
    return pl.pallas_call(
        functools.partial(_idx_sample_kernel, pages_per_step=pages_per_step, topk=topk,
                          pos_bits=int(n_pages * PAGE_SIZE).bit_length()),
        grid_spec=pltpu.PrefetchScalarGridSpec(
            num_scalar_prefetch=1,
            grid=(nb, n_pages // pages_per_step),
            in_specs=[row((nh, IDX_DIM)), row((nh, 1)), row((1, IDX_DIM))] + pspecs,
            out_specs=[row((n_pages, PAGE_SIZE)), row((8, LANES))],
        ),
        out_shape=[jax.ShapeDtypeStruct((nb, n_pages, PAGE_SIZE), I32),
                   jax.ShapeDtypeStruct((nb, 8, LANES), I32)],
        compiler_params=_params(("parallel", "arbitrary")),
    )(page_table, qi.reshape(nb, nh, IDX_DIM), wi.reshape(nb, nh, 1), ki_new.reshape(nb, 1, IDX_DIM),
      *([cache_t] * pages_per_step))


def _dsa_sample_kernel(pt_ref, key_ref, meta_ref, q_ref, kn_ref, vn_ref, *rest, n_heads, n_kv, pages_per_step):
    del pt_ref
    k_refs = rest[:pages_per_step]
    v_refs = rest[pages_per_step:2 * pages_per_step]
    o_ref, m_ref, l_ref, acc_ref = rest[2 * pages_per_step:]
    j = pl.program_id(1)
    rep = n_heads // n_kv

    @pl.when(j == 0)
    def _():
        m_ref[...] = jnp.full(m_ref.shape, SOFTMAX_M_INIT, F32)
        l_ref[...] = jnp.zeros_like(l_ref)
        acc_ref[...] = jnp.zeros_like(acc_ref)

    q = q_ref[...]
    meta = meta_ref[...]
    thr = meta[0:1, :]
    group = lax.broadcasted_iota(I32, (n_heads, PAGE_SIZE), 0) // rep
    m, l, acc = m_ref[...], l_ref[...], acc_ref[...]

    def update(m, l, acc, s, pv):
        m_new = jnp.maximum(m, jnp.max(s, axis=1, keepdims=True))
        p = jnp.exp(s - m_new)
        alpha = jnp.exp(m - m_new)
        return m_new, alpha * l + jnp.sum(p, axis=1, keepdims=True), alpha * acc + pv(p)

    def page_logits(p_i, k_ref):
        bias = jnp.where(key_ref[pl.ds(j * pages_per_step + p_i, 1), :] >= thr, 0.0, MASK_BIAS)
        s = jnp.zeros((n_heads, PAGE_SIZE), F32)
        for g in range(n_kv):
            kg = k_ref[pl.ds(g, PAGE_SIZE, stride=n_kv), :].astype(BF16)
            s = jnp.where(group == g, lax.dot_general(q, kg, _NT, preferred_element_type=F32), s)
        return s * SCALE + bias

    def pv(p):
        pb = p.astype(BF16)
        out = jnp.zeros((n_heads, HEAD_DIM), F32)
        for p_i, v_ref in enumerate(v_refs):
            page_p = pb[:, p_i * PAGE_SIZE:(p_i + 1) * PAGE_SIZE]
            for g in range(n_kv):
                vg = v_ref[pl.ds(g, PAGE_SIZE, stride=n_kv), :].astype(BF16)
                out = out + jnp.where(group == g, jnp.dot(page_p, vg, preferred_element_type=F32), 0.0)
        return out

    s_all = jnp.concatenate([page_logits(p_i, k_ref) for p_i, k_ref in enumerate(k_refs)], axis=1)
    m, l, acc = update(m, l, acc, s_all, pv)
    m_ref[...], l_ref[...], acc_ref[...] = m, l, acc

    @pl.when(j == pl.num_programs(1) - 1)
    def _():
        own_bias = jnp.where(meta[1:2, 0:1] > 0, 0.0, MASK_BIAS)
        s_own = jnp.sum(q.astype(F32) * kn_ref[...].astype(F32), axis=1, keepdims=True) * SCALE + own_bias
        vn = vn_ref[...].astype(F32)
        _, l_f, acc_f = update(m, l, acc, s_own, lambda p: p.astype(BF16).astype(F32) * vn)
        o_ref[...] = (acc_f / l_f).astype(BF16)


def _dsa_sample_call(page_table, keys, meta, qb, kb_new, vb_new, cache_k, cache_v, *, layer, pages_per_step):
    nb, n_pages = page_table.shape
    n_kv = cache_k.shape[3]
    n_heads = qb.shape[1] // HEAD_DIM
    rep = n_heads // n_kv
    row = lambda s: pl.BlockSpec((None,) + s, lambda b, j, pt: (b, 0, 0))
    own = lambda t: jnp.repeat(t.reshape(nb, n_kv, HEAD_DIM), rep, axis=1)
    page_block = (None, None, PAGE_SIZE * n_kv, HEAD_DIM)
    kspecs = _page_specs(page_block, n_pages, pages_per_step, layer, False)
    out = pl.pallas_call(
        functools.partial(_dsa_sample_kernel, n_heads=n_heads, n_kv=n_kv, pages_per_step=pages_per_step),
        grid_spec=pltpu.PrefetchScalarGridSpec(
            num_scalar_prefetch=1,
            grid=(nb, n_pages // pages_per_step),
            in_specs=[row((n_pages, PAGE_SIZE)), row((8, LANES)), row((n_heads, HEAD_DIM)),
                      row((n_heads, HEAD_DIM)), row((n_heads, HEAD_DIM))] + kspecs + kspecs,
            out_specs=row((n_heads, HEAD_DIM)),
            scratch_shapes=[pltpu.VMEM((n_heads, 1), F32), pltpu.VMEM((n_heads, 1), F32),
                            pltpu.VMEM((n_heads, HEAD_DIM), F32)],
        ),
        out_shape=jax.ShapeDtypeStruct((nb, n_heads, HEAD_DIM), BF16),
        compiler_params=_params(("parallel", "arbitrary")),
    )(page_table, keys, meta, qb.reshape(nb, n_heads, HEAD_DIM), own(kb_new), own(vb_new),
      *([_flat_pages(cache_k)] * pages_per_step), *([_flat_pages(cache_v)] * pages_per_step))
    return out.reshape(nb, n_heads * HEAD_DIM)


def _merge_kernel(x_ref, oa_ref, ob_ref, ga_ref, gb_ref, wa_ref, wb_ref, wo_ref, g_ref, y_ref, hn_ref):
    mixed = (ga_ref[...] * jnp.dot(oa_ref[...], wa_ref[...], preferred_element_type=F32)
             + gb_ref[...] * jnp.dot(ob_ref[...], wb_ref[...], preferred_element_type=F32))
    y = x_ref[...] + jnp.dot(mixed.astype(BF16), wo_ref[...], preferred_element_type=F32)
    y_ref[...] = y
    hn_ref[...] = _rms(y, g_ref[...]).astype(BF16)


def _merge_call(x, oa, ob, gates, wa, wb, wo, g_next, *, tm):
    n, d = x.shape
    row = lambda w: pl.BlockSpec((tm, w), lambda i: (i, 0))
    const = lambda s: pl.BlockSpec(s, lambda i: (0, 0), pipeline_mode=pl.Buffered(1))
    return pl.pallas_call(
        _merge_kernel,
        grid=(n // tm,),
        in_specs=[row(d), row(oa.shape[1]), row(ob.shape[1]),
                  pl.BlockSpec((tm, d), lambda i: (i, 0)), pl.BlockSpec((tm, d), lambda i: (i, 1)),
                  const(wa.shape), const(wb.shape), const(wo.shape), const((1, d))],
        out_specs=[row(d), row(d)],
        out_shape=[jax.ShapeDtypeStruct((n, d), F32), jax.ShapeDtypeStruct((n, d), BF16)],
        compiler_params=_params(("parallel",)),
    )(x, oa, ob, gates, gates, wa, wb, wo, g_next.reshape(1, d))


def _rope_tables(pos, head_dim):
    rot = head_dim // ROT_FRACTION
    half = rot // 2
    inv = 1.0 / (ROPE_THETA ** (jnp.arange(half, dtype=F32) * 2.0 / rot))
    ang = pos.astype(F32)[:, None] * inv[None, :]
    cos, sin = jnp.cos(ang), jnp.sin(ang)
    lane = np.arange(LANES) % head_dim
    first, second = lane < half, (lane >= half) & (lane < rot)
    freq = np.where(second, lane - half, np.where(first, lane, 0))
    cos_t = jnp.where(first | second, cos[:, freq], 1.0)
    sin_lo = jnp.where(first, -sin[:, freq], 0.0)
    sin_hi = jnp.where(second, sin[:, freq], 0.0)
    return cos_t, sin_lo, sin_hi


def _suffix_sum_rhs(n):
    strict_lower = (np.arange(n)[:, None] > np.arange(n)[None, :]).astype(np.float32)
    return np.concatenate([strict_lower, np.ones((n, n), np.float32)], axis=1)


def _row_tile(n, pref):
    return pref if n % pref == 0 else n


def kernel(x_prompt, x_sample, cache_sb_k, cache_sb_v, cache_dsa_k, cache_dsa_v, cache_idx_k, page_table, ffn1_g, ffn1_w1, ffn1_w3, ffn1_w2, mix_g, w_in, w_br_sb, w_br_dsa, w_out, ffn2_g, ffn2_w1, ffn2_w3, ffn2_w2, final_g):
    bsz, seq, d = x_prompt.shape
    nb, n_new, _ = x_sample.shape
    assert n_new == 1, "the sample path is written for one new token per sequence"
    depth = ffn1_g.shape[0]
    n_sb = cache_sb_k.shape[3]
    n_kv = cache_dsa_k.shape[3]
    w_sb = n_sb * HEAD_DIM
    w_dq = w_br_dsa.shape[1]
    n_dsa = w_dq // HEAD_DIM
    wkv = n_kv * HEAD_DIM
    n_idx = (w_in.shape[2] - 3 * w_sb - w_dq - 2 * wkv - IDX_DIM - 2 * d) // (IDX_DIM + 1)
    w_iq = n_idx * IDX_DIM
    assert w_sb == w_dq == w_iq and 2 * wkv + 2 * LANES <= w_sb
    n_pages = page_table.shape[1]
    past = n_pages * PAGE_SIZE
    topk_p = min(TOPK_MAX, seq // 4)
    topk_s = min(TOPK_MAX, (past + n_new) // 4)

    pos_p = jnp.arange(seq)
    pos_s = jnp.full((nb,), past, jnp.int32)
    tabs_p = _rope_tables(pos_p, HEAD_DIM) + _rope_tables(pos_p, IDX_DIM)
    tabs_s = _rope_tables(pos_s, HEAD_DIM) + _rope_tables(pos_s, IDX_DIM)
    tq_sb = _row_tile(seq, 128)
    sum_rhs_q = jnp.asarray(_suffix_sum_rhs(tq_sb), BF16)
    sum_rhs_page = jnp.asarray(_suffix_sum_rhs(PAGE_SIZE), BF16)

    xp = x_prompt.reshape(bsz * seq, d)
    xs = x_sample.reshape(nb, d)
    tm_p = _row_tile(bsz * seq, 512)
    tm_in = _row_tile(seq, 512)
    tm_merge = _row_tile(bsz * seq, 256)
    tf = _row_tile(ffn1_w1.shape[2], 512)
    st_p = [[] for _ in range(5)]
    st_s = [[] for _ in range(5)]

    hp = _rms_call(xp, ffn1_g[0], tm_p)
    hs = _rms_call(xs, ffn1_g[0], nb)
    for layer in range(depth):
        bf = lambda w: w[layer].astype(BF16)
        w1a, w3a, w2a = bf(ffn1_w1), bf(ffn1_w3), bf(ffn1_w2)
        w1b, w3b, w2b = bf(ffn2_w1), bf(ffn2_w3), bf(ffn2_w2)
        wl = w_in[layer]
        off = np.cumsum([0, w_sb, w_sb, w_sb, w_dq, wkv, wkv, w_iq, IDX_DIM, n_idx, d, d])
        col = lambda k: wl[:, off[k]:off[k + 1]]
        pad = lambda w: jnp.zeros((d, w), F32)
        misc = jnp.concatenate([col(4), col(5), col(7), pad(LANES - IDX_DIM), col(8), pad(LANES - n_idx),
                                pad(w_sb - 2 * wkv - 2 * LANES)], axis=1)
        w_tiles = jnp.stack([col(0), col(1), col(2), col(3), col(6), misc]).astype(BF16)
        w_gates = wl[:, off[9]:off[11]].astype(BF16)
        wa, wb, wo = bf(w_br_sb), bf(w_br_dsa), bf(w_out)

        xp, hp = _ffn_call(xp, hp, w1a, w3a, w2a, mix_g[layer], final=False, tm=tm_p, tf=tf)
        xs, hs = _ffn_call(xs, hs, w1a, w3a, w2a, mix_g[layer], final=False, tm=nb, tf=tf)

        (qa, ka, kab, va, vab, qb_t, qi_t, kb, kbb, vb, vb_t, ki, kib, wi_t) = _inproj_call(
            hp, w_tiles, tabs_p, tm=tm_in, n_tab_blocks=seq // tm_in, wkv=wkv, n_idx_heads=n_idx,
            query_on_lanes=True)
        gates = _gates_call(hp, w_gates, tm=tm_p, tn=_row_tile(2 * d, 1024))
        oa = _sb_prompt_call(qa, kab, vab, sum_rhs_q, bsz=bsz, seq=seq, n_heads=n_sb, tq=tq_sb)
        ob = _dsa_prompt_call(qi_t, wi_t, kib, qb_t, kbb, vb_t, bsz=bsz, seq=seq, n_idx_heads=n_idx,
                              n_heads=n_dsa, n_kv=n_kv, topk=topk_p, tq=_row_tile(seq, 128))
        xp, hp = _merge_call(xp, oa, ob, gates, wa, wb, wo, ffn2_g[layer], tm=tm_merge)
        for store, val, shape in zip(st_p, (ka, va, kb, vb, ki),
                                     ((n_sb, HEAD_DIM), (n_sb, HEAD_DIM), (n_kv, HEAD_DIM), (n_kv, HEAD_DIM), (IDX_DIM,))):
            store.append(val.reshape((bsz, seq) + shape))

        (qa, ka, kab, va, vab, qb, qi, kb, kbb, vb, vbb, ki, kib, wi) = _inproj_call(
            hs, w_tiles, tabs_s, tm=nb, n_tab_blocks=1, wkv=wkv, n_idx_heads=n_idx, query_on_lanes=False)
        gates = _gates_call(hs, w_gates, tm=nb, tn=_row_tile(2 * d, 1024))
        oa = _sb_sample_call(page_table, qa, cache_sb_k, cache_sb_v, sum_rhs_page, layer=layer)
        keys, meta = _idx_sample_call(page_table, qi, wi, kib, cache_idx_k, layer=layer,
                                      pages_per_step=16 if n_pages % 16 == 0 else 1, topk=topk_s)
        ob = _dsa_sample_call(page_table, keys, meta, qb, kbb, vbb, cache_dsa_k, cache_dsa_v, layer=layer,
                              pages_per_step=8 if n_pages % 8 == 0 else 1)
        xs, hs = _merge_call(xs, oa, ob, gates, wa, wb, wo, ffn2_g[layer], tm=nb)
        for store, val, shape in zip(st_s, (ka, va, kb, vb, ki),
                                     ((n_sb, HEAD_DIM), (n_sb, HEAD_DIM), (n_kv, HEAD_DIM), (n_kv, HEAD_DIM), (IDX_DIM,))):
            store.append(val.reshape((nb, n_new) + shape))

        last = layer == depth - 1
        g_next = final_g if last else ffn1_g[layer + 1]
        if last:
            xp = _ffn_call(xp, hp, w1b, w3b, w2b, g_next, final=True, tm=tm_p, tf=tf)
            xs = _ffn_call(xs, hs, w1b, w3b, w2b, g_next, final=True, tm=nb, tf=tf)
        else:
            xp, hp = _ffn_call(xp, hp, w1b, w3b, w2b, g_next, final=False, tm=tm_p, tf=tf)
            xs, hs = _ffn_call(xs, hs, w1b, w3b, w2b, g_next, final=False, tm=nb, tf=tf)

    return (xp.reshape(bsz, seq, d), xs.reshape(nb, n_new, d),
            jnp.stack(st_p[0]), jnp.stack(st_p[1]), jnp.stack(st_p[2]), jnp.stack(st_p[3]), jnp.stack(st_p[4]),
            jnp.stack(st_s[0]), jnp.stack(st_s[1]), jnp.stack(st_s[2]), jnp.stack(st_s[3]), jnp.stack(st_s[4]))
```

```python
import functools

import jax
import jax.numpy as jnp
import numpy as np
from jax import lax
from jax.experimental import pallas as pl
from jax.experimental.pallas import tpu as pltpu

HEAD_DIM = 128
IDX_DIM = 64
TOPK_MAX = 256
ROPE_THETA = 500000.0
ROT_FRACTION = 4
EPS = 1e-6
PAGE_SIZE = 128
SCALE = HEAD_DIM ** -0.5
LOG2_E = 1.4426950408889634

LANES = 128
VMEM_LIMIT = 56 * 1024 * 1024

BF16_ROWS = 16
INT_MIN = -(2 ** 31)
HIGH_HALF = -65536
MIN_NORMAL_BITS = 0x00800000
NEG_KEY = -2139095041
SOFTMAX_M_INIT = -1e30
MASK_BIAS = -2e30
SB_LOG_UNDERFLOW = -110.0

F32 = jnp.float32
BF16 = jnp.bfloat16
I32 = jnp.int32

_NT = (((1,), (1,)), ((), ()))


def _params(sem):
    return pltpu.CompilerParams(dimension_semantics=sem, vmem_limit_bytes=VMEM_LIMIT)


def _rms(x, g):
    return x * lax.rsqrt(jnp.mean(x * x, axis=-1, keepdims=True) + EPS) * g


def _log_sigmoid(z):
    return jnp.minimum(z, 0.0) - jnp.log(1.0 + jnp.exp(-jnp.abs(z)))


def _flushed_bits(x):
    bits = pltpu.bitcast(x, I32)
    return jnp.where((bits & 0x7F800000) == 0, 0, bits)


def _key_of_bits(bits):
    return jnp.where(bits < 0, bits ^ 0x7FFFFFFF, bits)


def _order_key(x):
    return _key_of_bits(_flushed_bits(x))


def _tree_sum(terms):
    while len(terms) > 1:
        terms = [a + b for a, b in zip(terms[0::2], terms[1::2])] + (terms[-1:] if len(terms) % 2 else [])
    return terms[0]


def _split_bf16(x):
    hi = x.astype(BF16)
    lo = (x - hi.astype(F32)).astype(BF16)
    return hi, lo


def _rms_kernel(x_ref, g_ref, h_ref):
    h_ref[...] = _rms(x_ref[...], g_ref[...]).astype(BF16)


def _rms_call(x, g, tm):
    n, d = x.shape
    return pl.pallas_call(
        _rms_kernel,
        grid=(n // tm,),
        in_specs=[pl.BlockSpec((tm, d), lambda i: (i, 0)), pl.BlockSpec((1, d), lambda i: (0, 0))],
        out_specs=pl.BlockSpec((tm, d), lambda i: (i, 0)),
        out_shape=jax.ShapeDtypeStruct((n, d), BF16),
        compiler_params=_params(("parallel",)),
    )(x, g.reshape(1, d))


def _ffn_kernel(x_ref, h_ref, w1_ref, w3_ref, w2_ref, g_ref, *rest, final):
    if final:
        y_ref, acc_ref = rest
    else:
        y_ref, hn_ref, acc_ref = rest
    f = pl.program_id(1)

    @pl.when(f == 0)
    def _():
        acc_ref[...] = jnp.zeros_like(acc_ref)

    h = h_ref[...]
    a = jnp.dot(h, w1_ref[...], preferred_element_type=F32)
    b = jnp.dot(h, w3_ref[...], preferred_element_type=F32)
    u = (a * jax.nn.sigmoid(a)) * b
    acc_ref[...] += jnp.dot(u.astype(BF16), w2_ref[...], preferred_element_type=F32)

    @pl.when(f == pl.num_programs(1) - 1)
    def _():
        y = x_ref[...] + 0.5 * acc_ref[...]
        if final:
            y_ref[...] = _rms(y, g_ref[...])
        else:
            y_ref[...] = y
            hn_ref[...] = _rms(y, g_ref[...]).astype(BF16)


def _ffn_call(x, h, w1, w3, w2, g_next, *, final, tm, tf):
    n, d = x.shape
    dff = w1.shape[1]
    out_shape = [jax.ShapeDtypeStruct((n, d), F32)]
    out_specs = [pl.BlockSpec((tm, d), lambda i, f: (i, 0))]
    if not final:
        out_shape.append(jax.ShapeDtypeStruct((n, d), BF16))
        out_specs.append(pl.BlockSpec((tm, d), lambda i, f: (i, 0)))
    res = pl.pallas_call(
        functools.partial(_ffn_kernel, final=final),
        grid=(n // tm, dff // tf),
        in_specs=[
            pl.BlockSpec((tm, d), lambda i, f: (i, 0)),
            pl.BlockSpec((tm, d), lambda i, f: (i, 0)),
            pl.BlockSpec((d, tf), lambda i, f: (0, f)),
            pl.BlockSpec((d, tf), lambda i, f: (0, f)),
            pl.BlockSpec((tf, d), lambda i, f: (f, 0)),
            pl.BlockSpec((1, d), lambda i, f: (0, 0)),
        ],
        out_specs=out_specs,
        out_shape=out_shape,
        scratch_shapes=[pltpu.VMEM((tm, d), F32)],
        compiler_params=_params(("parallel", "arbitrary")),
    )(x, h, w1, w3, w2, g_next.reshape(1, d))
    return res[0] if final else res


def _rope(x, cos, sin_lo, sin_hi, half):
    return (x * cos + pltpu.roll(x, LANES - half, axis=1) * sin_lo
            + pltpu.roll(x, half, axis=1) * sin_hi)


def _inproj_kernel(h_ref, w_ref, cd_ref, sdl_ref, sdh_ref, ci_ref, sil_ref, sih_ref,
                   qa_ref, ka_ref, kab_ref, va_ref, vab_ref, qb_ref, qi_ref,
                   kb_ref, kbb_ref, vb_ref, vbb_ref, ki_ref, kib_ref, wi_ref, *, wkv, query_on_lanes):
    j = pl.program_id(1)
    acc = jnp.dot(h_ref[...], w_ref[0], preferred_element_type=F32)
    lay = (lambda x: x.T) if query_on_lanes else (lambda x: x)
    rot_d = HEAD_DIM // ROT_FRACTION // 2
    rot_i = IDX_DIM // ROT_FRACTION // 2

    def rope_cols(x, idx):
        if idx:
            c, sl, sh, half = ci_ref[...], sil_ref[...], sih_ref[...], rot_i
        else:
            c, sl, sh, half = cd_ref[...], sdl_ref[...], sdh_ref[...], rot_d
        return jnp.concatenate(
            [_rope(x[:, s:s + LANES], c, sl, sh, half) for s in range(0, x.shape[1], LANES)], axis=1)

    @pl.when(j == 0)
    def _():
        qa_ref[...] = acc.astype(BF16)

    @pl.when(j == 1)
    def _():
        ka_ref[...] = acc
        kab_ref[...] = acc.astype(BF16)

    @pl.when(j == 2)
    def _():
        va_ref[...] = acc
        vab_ref[...] = acc.astype(BF16)

    @pl.when(j == 3)
    def _():
        qb_ref[...] = lay(rope_cols(acc, False)).astype(BF16)

    @pl.when(j == 4)
    def _():
        qi_ref[...] = lay(rope_cols(acc, True)).astype(BF16)

    @pl.when(j == 5)
    def _():
        kb = rope_cols(acc[:, :wkv], False)
        kb_ref[...] = kb
        kbb_ref[...] = kb.astype(BF16)
        vb = acc[:, wkv:2 * wkv]
        vb_ref[...] = vb
        ki = rope_cols(acc[:, 2 * wkv:2 * wkv + LANES], True)[:, :IDX_DIM]
        ki_ref[...] = ki
        kib_ref[...] = ki.astype(BF16)
        wi_lanes = acc[:, 2 * wkv + LANES:2 * wkv + 2 * LANES]
        if query_on_lanes:
            vbb_ref[0] = vb.T.astype(BF16)
            wi_ref[...] = wi_lanes.T[:wi_ref.shape[0], :]
        else:
            vbb_ref[...] = vb.astype(BF16)
            wi_ref[...] = wi_lanes[:, :wi_ref.shape[1]]


def _inproj_call(h, w_tiles, tabs, *, tm, n_tab_blocks, wkv, n_idx_heads, query_on_lanes):
    n, d = h.shape
    nt, _, tw = w_tiles.shape
    row = lambda i, j: (i, 0)
    tab = lambda i, j: (i % n_tab_blocks, 0)
    rows_major = lambda w, dt: (jax.ShapeDtypeStruct((n, w), dt), pl.BlockSpec((tm, w), row))
    if query_on_lanes:
        lanes_major = lambda w, dt: (jax.ShapeDtypeStruct((w, n), dt), pl.BlockSpec((w, tm), lambda i, j: (0, i)))
        v_blocks = (jax.ShapeDtypeStruct((n // tm, wkv, tm), BF16), pl.BlockSpec((1, wkv, tm), lambda i, j: (i, 0, 0)))
    else:
        lanes_major = rows_major
        v_blocks = rows_major(wkv, BF16)
    outs = [rows_major(tw, BF16), rows_major(tw, F32), rows_major(tw, BF16), rows_major(tw, F32),
            rows_major(tw, BF16), lanes_major(tw, BF16), lanes_major(tw, BF16),
            rows_major(wkv, F32), rows_major(wkv, BF16), rows_major(wkv, F32), v_blocks,
            rows_major(IDX_DIM, F32), rows_major(IDX_DIM, BF16), lanes_major(n_idx_heads, F32)]
    return pl.pallas_call(
        functools.partial(_inproj_kernel, wkv=wkv, query_on_lanes=query_on_lanes),
        grid=(n // tm, nt),
        in_specs=[pl.BlockSpec((tm, d), row), pl.BlockSpec((1, d, tw), lambda i, j: (j, 0, 0))]
        + [pl.BlockSpec((tm, LANES), tab)] * 6,
        out_specs=[spec for _, spec in outs],
        out_shape=[shape for shape, _ in outs],
        compiler_params=_params(("parallel", "arbitrary")),
    )(h, w_tiles, *tabs)


def _gates_kernel(h_ref, w_ref, o_ref):
    o_ref[...] = jax.nn.sigmoid(jnp.dot(h_ref[...], w_ref[...], preferred_element_type=F32))


def _gates_call(h, wg, *, tm, tn):
    n, d = h.shape
    ncol = wg.shape[1]
    return pl.pallas_call(
        _gates_kernel,
        grid=(n // tm, ncol // tn),
        in_specs=[pl.BlockSpec((tm, d), lambda i, j: (i, 0)), pl.BlockSpec((d, tn), lambda i, j: (0, j))],
        out_specs=pl.BlockSpec((tm, tn), lambda i, j: (i, j)),
        out_shape=jax.ShapeDtypeStruct((n, ncol), F32),
        compiler_params=_params(("parallel", "arbitrary")),
    )(h, wg)


def _sb_weights(z, sum_rhs, carry, mask):
    n = z.shape[1]
    ls = _log_sigmoid(z)
    lr = ls - z
    if mask is not None:
        lr = jnp.where(mask, lr, 0.0)
    hi, lo = _split_bf16(lr)
    sums = (jnp.dot(hi, sum_rhs, preferred_element_type=F32)
            + jnp.dot(lo, sum_rhs, preferred_element_type=F32))
    a = jnp.exp(ls + sums[:, :n] + carry)
    if mask is not None:
        a = jnp.where(mask, a, 0.0)
    return a, carry + sums[:, n:]


def _sb_prompt_kernel(q_ref, k_ref, v_ref, rhs_ref, o_ref, carry_ref, acc_ref, *, tq, n_heads):
    qi = pl.program_id(1)
    rhs = rhs_ref[...]
    rows = lax.broadcasted_iota(I32, (n_heads * tq, tq), 0) % tq
    cols = lax.broadcasted_iota(I32, (n_heads * tq, tq), 1)
    head_lanes = [slice(h * HEAD_DIM, (h + 1) * HEAD_DIM) for h in range(n_heads)]

    def block(kb, mask, first):
        ks = pl.multiple_of(kb * tq, tq)
        z = jnp.concatenate(
            [lax.dot_general(q_ref[:, hl], k_ref[pl.ds(ks, tq), hl], _NT, preferred_element_type=F32)
             for hl in head_lanes], axis=0) * SCALE
        a, carry = _sb_weights(z, rhs, 0.0 if first else carry_ref[...], mask)
        carry_ref[...] = carry
        ab = a.astype(BF16)
        pv = jnp.concatenate(
            [jnp.dot(ab[h * tq:(h + 1) * tq], v_ref[pl.ds(ks, tq), hl], preferred_element_type=F32)
             for h, hl in enumerate(head_lanes)], axis=0)
        acc_ref[...] = pv if first else acc_ref[...] + pv
        return jnp.max(carry)

    def more(state):
        return (state[0] >= 0) & (state[1] > SB_LOG_UNDERFLOW)

    def step(state):
        return state[0] - 1, block(state[0], None, False)

    lax.while_loop(more, step, (qi - 1, block(qi, cols < rows, True)))
    for h, hl in enumerate(head_lanes):
        o_ref[:, hl] = acc_ref[h * tq:(h + 1) * tq, :].astype(BF16)


def _sb_prompt_call(qa, ka, va, sum_rhs, *, bsz, seq, n_heads, tq):
    width = n_heads * HEAD_DIM
    q3 = qa.reshape(bsz, seq, width)
    k3 = ka.reshape(bsz, seq, width)
    v3 = va.reshape(bsz, seq, width)
    qspec = pl.BlockSpec((None, tq, width), lambda b, i: (b, i, 0))
    kspec = pl.BlockSpec((None, seq, width), lambda b, i: (b, 0, 0), pipeline_mode=pl.Buffered(1))
    out = pl.pallas_call(
        functools.partial(_sb_prompt_kernel, tq=tq, n_heads=n_heads),
        grid=(bsz, seq // tq),
        in_specs=[qspec, kspec, kspec, pl.BlockSpec((tq, 2 * tq), lambda b, i: (0, 0))],
        out_specs=qspec,
        out_shape=jax.ShapeDtypeStruct((bsz, seq, width), BF16),
        scratch_shapes=[pltpu.VMEM((n_heads * tq, tq), F32), pltpu.VMEM((n_heads * tq, HEAD_DIM), F32)],
        compiler_params=_params(("parallel", "arbitrary")),
    )(q3, k3, v3, sum_rhs)
    return out.reshape(bsz * seq, width)


def _kth_largest_key(count_ge, shape, topk, count_ge_high=None):
    def step(count):
        def body(it, pat):
            cand = pat ^ jnp.left_shift(jnp.int32(1), 31 - it)
            return jnp.where(count(cand) >= topk, cand, pat)
        return body

    pat = lax.fori_loop(0, 16, step(count_ge_high or count_ge), jnp.full(shape, INT_MIN, I32))
    return lax.fori_loop(16, 32, step(count_ge), pat)


def _tie_cut(count_eq_before, need, shape, nbits):
    def body(it, c):
        cand = c | jnp.left_shift(jnp.int32(1), nbits - 1 - it)
        return jnp.where(count_eq_before(cand) < need, cand, c)

    return lax.fori_loop(0, nbits, body, jnp.zeros(shape, I32)) + 1


def _dsa_prompt_kernel(qi_ref, wi_ref, ki_ref, qb_ref, kb_ref, vb_ref, o_ref,
                       key_ref, high_ref, m_ref, l_ref, acc_ref, *, tq, tk, topk, n_idx_heads, n_kv, rep, pos_bits):
    i = pl.program_id(1)
    q0 = i * tq
    nkb = (q0 + tq + tk - 1) // tk
    kpos = lax.broadcasted_iota(I32, (tk, tq), 0)
    qpos = q0 + lax.broadcasted_iota(I32, (tk, tq), 1)

    pair_rhs = [jnp.concatenate([qi_ref[h * IDX_DIM:(h + 1) * IDX_DIM, :],
                                 qi_ref[(h + 1) * IDX_DIM:(h + 2) * IDX_DIM, :]], axis=1)
                for h in range(0, n_idx_heads, 2)]
    w_row = [wi_ref[h:h + 1, :] for h in range(n_idx_heads)]

    def score_block(kb, _):
        ks = pl.multiple_of(kb * tk, tk)
        kblk = ki_ref[pl.ds(ks, tk), :]
        sc = jnp.zeros((tk, tq), F32)
        for p, rhs in enumerate(pair_rhs):
            logit = jnp.dot(kblk, rhs, preferred_element_type=F32)
            sc = (sc + jnp.maximum(logit[:, :tq], 0.0) * w_row[2 * p]
                  + jnp.maximum(logit[:, tq:], 0.0) * w_row[2 * p + 1])
        sc = jnp.where(ks + kpos <= qpos, sc, -jnp.inf)
        bits = _flushed_bits(sc)
        key_ref[pl.ds(ks, tk), :] = _key_of_bits(bits)
        high_ref[pl.ds(ks, tk), :] = pltpu.bitcast(bits & HIGH_HALF, F32).astype(BF16)
        return 0

    lax.fori_loop(0, nkb, score_block, 0)

    def count(pred):
        def body(kb, cnt):
            ks = pl.multiple_of(kb * tk, tk)
            hit = pred(key_ref[pl.ds(ks, tk), :], ks + kpos).astype(I32)
            return cnt + jnp.sum(hit.reshape(tk // 8, 8, tq), axis=0)

        cnt = lax.fori_loop(0, nkb, body, jnp.zeros((8, tq), I32))
        return jnp.sum(cnt, axis=0, keepdims=True)

    count_ge = lambda v: count(lambda key, pos: key >= v)

    def count_ge_high(v):
        bits = jnp.where(v < 0, v ^ (0x7FFFFFFF & HIGH_HALF), jnp.where((v > 0) & (v < MIN_NORMAL_BITS),
                                                                        MIN_NORMAL_BITS, v))
        cand = pltpu.bitcast(bits, F32).astype(BF16)

        def body(kb, cnt):
            ks = pl.multiple_of(kb * tk, tk)
            hit = jnp.where(high_ref[pl.ds(ks, tk), :] >= cand, jnp.ones((), BF16), jnp.zeros((), BF16))
            part = _tree_sum([hit[r:r + BF16_ROWS] for r in range(0, tk, BF16_ROWS)])
            return cnt + part.astype(F32)

        cnt = lax.fori_loop(0, nkb, body, jnp.zeros((BF16_ROWS, tq), F32))
        return jnp.sum(cnt, axis=0, keepdims=True).astype(I32)

    kth = _kth_largest_key(count_ge, (1, tq), topk, count_ge_high)
    thr = jnp.maximum(kth, NEG_KEY + 1)

    n_ge = count_ge(thr)

    @pl.when(jnp.max(n_ge) > topk)
    def _():
        need = topk - count(lambda key, pos: key > thr)
        cut = _tie_cut(lambda c: count(lambda key, pos: (key == thr) & (pos < c)), need, (1, tq), pos_bits)

        def drop(kb, _):
            ks = pl.multiple_of(kb * tk, tk)
            key = key_ref[pl.ds(ks, tk), :]
            key_ref[pl.ds(ks, tk), :] = jnp.where((key == thr) & (ks + kpos >= cut), NEG_KEY, key)
            return 0

        lax.fori_loop(0, nkb, drop, 0)

    m_ref[...] = jnp.full(m_ref.shape, SOFTMAX_M_INIT, F32)
    l_ref[...] = jnp.zeros_like(l_ref)
    acc_ref[...] = jnp.zeros_like(acc_ref)
    q_cols = [jnp.concatenate([qb_ref[(g * rep + r) * HEAD_DIM:(g * rep + r + 1) * HEAD_DIM, :]
                               for r in range(rep)], axis=1) for g in range(n_kv)]

    def attend(kb, _):
        ks = pl.multiple_of(kb * tk, tk)
        bias = jnp.where(key_ref[pl.ds(ks, tk), :] >= thr, 0.0, MASK_BIAS)
        bias = jnp.concatenate([bias] * rep, axis=1)
        s = [jnp.dot(kb_ref[pl.ds(ks, tk), g * HEAD_DIM:(g + 1) * HEAD_DIM], q_cols[g],
                     preferred_element_type=F32) * (SCALE * LOG2_E) + bias for g in range(n_kv)]
        m_old = [m_ref[g] for g in range(n_kv)]
        m_new = [jnp.maximum(m_old[g], jnp.max(s[g], axis=0, keepdims=True)) for g in range(n_kv)]
        p = [jnp.exp2(s[g] - m_new[g]) for g in range(n_kv)]
        for g in range(n_kv):
            alpha = jnp.exp2(m_old[g] - m_new[g])
            l_ref[g] = alpha * l_ref[g] + jnp.sum(p[g], axis=0, keepdims=True)
            v_t = vb_ref[kb, g * HEAD_DIM:(g + 1) * HEAD_DIM, :]
            acc_ref[g] = alpha * acc_ref[g] + jnp.dot(v_t, p[g].astype(BF16), preferred_element_type=F32)
            m_ref[g] = m_new[g]
        return 0

    lax.fori_loop(0, nkb, attend, 0)
    for g in range(n_kv):
        out_t = acc_ref[g] / l_ref[g]
        for r in range(rep):
            h = g * rep + r
            o_ref[:, h * HEAD_DIM:(h + 1) * HEAD_DIM] = out_t[:, r * tq:(r + 1) * tq].T.astype(BF16)


def _dsa_prompt_call(qi_t, wi_t, ki, qb_t, kb, vb_t, *, bsz, seq, n_idx_heads, n_heads, n_kv, topk, tq):
    rep = n_heads // n_kv
    nq = seq // tq
    tk = vb_t.shape[2]
    wkv = n_kv * HEAD_DIM
    wq, wb = n_idx_heads * IDX_DIM, n_heads * HEAD_DIM
    qspec = lambda w: pl.BlockSpec((w, tq), lambda b, i: (0, b * nq + i))
    kspec = lambda w: pl.BlockSpec((None, seq, w), lambda b, i: (b, 0, 0))
    out = pl.pallas_call(
        functools.partial(_dsa_prompt_kernel, tq=tq, tk=tk, topk=topk, n_idx_heads=n_idx_heads,
                          n_kv=n_kv, rep=rep, pos_bits=max(1, int(seq - 1).bit_length())),
        grid=(bsz, nq),
        in_specs=[qspec(wq), qspec(n_idx_heads), kspec(IDX_DIM), qspec(wb), kspec(wkv),
                  pl.BlockSpec((seq // tk, wkv, tk), lambda b, i: (b, 0, 0))],
        out_specs=pl.BlockSpec((None, tq, wb), lambda b, i: (b, i, 0)),
        out_shape=jax.ShapeDtypeStruct((bsz, seq, wb), BF16),
        scratch_shapes=[pltpu.VMEM((seq, tq), I32), pltpu.VMEM((seq, tq), BF16),
                        pltpu.VMEM((n_kv, 1, rep * tq), F32),
                        pltpu.VMEM((n_kv, 1, rep * tq), F32),
                        pltpu.VMEM((n_kv, HEAD_DIM, rep * tq), F32)],
        compiler_params=_params(("parallel", "arbitrary")),
    )(qi_t, wi_t, ki.reshape(bsz, seq, IDX_DIM), qb_t, kb.reshape(bsz, seq, wkv), vb_t)
    return out.reshape(bsz * seq, wb)


def _sb_sample_kernel(pt_ref, q_ref, rhs_ref, ck_ref, cv_ref, o_ref, kbuf, vbuf, sem, *, layer, n_heads, n_pages):
    b = pl.program_id(0)
    q = q_ref[...]
    rhs = rhs_ref[...]
    head = lax.broadcasted_iota(I32, (n_heads, PAGE_SIZE), 0)

    def page_copies(n, slot):
        page = pt_ref[b, n_pages - 1 - n]
        return (pltpu.make_async_copy(ck_ref.at[layer, page], kbuf.at[slot], sem.at[0, slot]),
                pltpu.make_async_copy(cv_ref.at[layer, page], vbuf.at[slot], sem.at[1, slot]))

    def start(n, slot):
        for copy in page_copies(n, slot):
            copy.start()

    def wait(n, slot):
        for copy in page_copies(n, slot):
            copy.wait()

    def more(state):
        return (state[0] < n_pages) & (state[1] > SB_LOG_UNDERFLOW)

    def step(state):
        n, _, carry, acc = state
        slot = n % 2
        wait(n, slot)

        @pl.when(n + 1 < n_pages)
        def _():
            start(n + 1, 1 - slot)

        z = jnp.zeros((n_heads, PAGE_SIZE), F32)
        for h in range(n_heads):
            kh = kbuf[slot, pl.ds(h, PAGE_SIZE, stride=n_heads), :].astype(BF16)
            z = jnp.where(head == h, lax.dot_general(q, kh, _NT, preferred_element_type=F32), z)
        a, carry = _sb_weights(z * SCALE, rhs, carry, None)
        ab = a.astype(BF16)
        for h in range(n_heads):
            vh = vbuf[slot, pl.ds(h, PAGE_SIZE, stride=n_heads), :].astype(BF16)
            acc = acc + jnp.where(head == h, jnp.dot(ab, vh, preferred_element_type=F32), 0.0)
        return n + 1, jnp.max(carry), carry, acc

    start(0, 0)
    n_done, _, _, acc = lax.while_loop(
        more, step, (jnp.int32(0), jnp.float32(0.0), jnp.zeros((n_heads, PAGE_SIZE), F32),
                     jnp.zeros((n_heads, HEAD_DIM), F32)))

    @pl.when(n_done < n_pages)
    def _():
        wait(n_done, n_done % 2)

    o_ref[...] = acc.astype(BF16)


def _page_specs(block, n_steps_pages, pages_per_step, layer, reverse):
    specs = []
    for p in range(pages_per_step):
        if reverse:
            pick = lambda b, j, pt, p=p: pt[b, n_steps_pages - 1 - (j * pages_per_step + p)]
        else:
            pick = lambda b, j, pt, p=p: pt[b, j * pages_per_step + p]
        zeros = (0,) * (len(block) - 2)
        specs.append(pl.BlockSpec(block, lambda b, j, pt, pick=pick: (layer, pick(b, j, pt)) + zeros))
    return specs


def _flat_pages(cache):
    depth, pool, page, heads, dim = cache.shape
    return cache.reshape(depth, pool, page * heads, dim)


def _sb_sample_call(page_table, qa, cache_k, cache_v, sum_rhs, *, layer):
    nb, n_pages = page_table.shape
    n_heads = cache_k.shape[3]
    q3 = qa.reshape(nb, n_heads, HEAD_DIM)
    row = pl.BlockSpec((None, n_heads, HEAD_DIM), lambda b, pt: (b, 0, 0))
    page_rows = PAGE_SIZE * n_heads
    out = pl.pallas_call(
        functools.partial(_sb_sample_kernel, layer=layer, n_heads=n_heads, n_pages=n_pages),
        grid_spec=pltpu.PrefetchScalarGridSpec(
            num_scalar_prefetch=1,
            grid=(nb,),
            in_specs=[row, pl.BlockSpec((PAGE_SIZE, 2 * PAGE_SIZE), lambda b, pt: (0, 0)),
                      pl.BlockSpec(memory_space=pl.ANY), pl.BlockSpec(memory_space=pl.ANY)],
            out_specs=row,
            scratch_shapes=[pltpu.VMEM((2, page_rows, HEAD_DIM), F32), pltpu.VMEM((2, page_rows, HEAD_DIM), F32),
                            pltpu.SemaphoreType.DMA((2, 2))],
        ),
        out_shape=jax.ShapeDtypeStruct((nb, n_heads, HEAD_DIM), BF16),
        compiler_params=_params(("arbitrary",)),
    )(page_table, q3, sum_rhs, _flat_pages(cache_k), _flat_pages(cache_v))
    return out.reshape(nb, n_heads * HEAD_DIM)


def _idx_sample_kernel(pt_ref, qi_ref, wi_ref, kin_ref, *rest, pages_per_step, topk, pos_bits):
    del pt_ref
    page_refs = rest[:pages_per_step]
    key_ref, meta_ref = rest[pages_per_step:]
    j = pl.program_id(1)
    qi = qi_ref[...]
    wi = wi_ref[...]
    logit = jnp.concatenate(
        [jnp.dot(qi, page_ref[...].astype(BF16), preferred_element_type=F32)
         for page_ref in page_refs], axis=1)
    keys = _order_key(jnp.sum(jnp.maximum(logit, 0.0) * wi, axis=0, keepdims=True))
    for p in range(pages_per_step):
        key_ref[pl.ds(j * pages_per_step + p, 1), :] = keys[:, p * PAGE_SIZE:(p + 1) * PAGE_SIZE]

    @pl.when(j == pl.num_programs(1) - 1)
    def _():
        n_pages = key_ref.shape[0]
        past = n_pages * PAGE_SIZE
        pos = (lax.broadcasted_iota(I32, (n_pages, PAGE_SIZE), 0) * PAGE_SIZE
               + lax.broadcasted_iota(I32, (n_pages, PAGE_SIZE), 1))
        own = jnp.sum(qi.astype(F32) * kin_ref[...].astype(F32), axis=1, keepdims=True)
        own_key = _order_key(jnp.sum(jnp.maximum(own, 0.0) * wi, axis=0, keepdims=True))

        def total(x):
            return jnp.sum(jnp.sum(x.astype(I32), axis=1, keepdims=True), axis=0, keepdims=True)

        def count(pred):
            return total(pred(key_ref[...], pos)) + pred(own_key, past).astype(I32)

        count_ge = lambda v: count(lambda key, where: key >= v)
        thr = _kth_largest_key(count_ge, (1, 1), topk)
        need = topk - count(lambda key, where: key > thr)
        cut = _tie_cut(lambda c: count(lambda key, where: (key == thr) & (where < c)), need, (1, 1), pos_bits)
        keys = key_ref[...]
        key_ref[...] = jnp.where((keys == thr) & (pos >= cut), NEG_KEY, keys)
        own_sel = (own_key > thr) | ((own_key == thr) & (past < cut))
        rowid = lax.broadcasted_iota(I32, meta_ref.shape, 0)
        meta_ref[...] = jnp.where(rowid == 0, thr, own_sel.astype(I32))


def _idx_sample_call(page_table, qi, wi, ki_new, cache_idx, *, layer, pages_per_step, topk):
    nb, n_pages = page_table.shape
    nh = wi.shape[1]
    row = lambda s: pl.BlockSpec((None,) + s, lambda b, j, pt: (b, 0, 0))
    cache_t = jnp.swapaxes(cache_idx, 2, 3)
    pspecs = _page_specs((None, None, IDX_DIM, PAGE_SIZE), n_pages, pages_per_step, layer, False)
    return pl.pallas_call(
        functools.partial(_idx_sample_kernel, pages_per_step=pages_per_step, topk=topk,
                          pos_bits=int(n_pages * PAGE_SIZE).bit_length()),
        grid_spec=pltpu.PrefetchScalarGridSpec(
            num_scalar_prefetch=1,
            grid=(nb, n_pages // pages_per_step),
            in_specs=[row((nh, IDX_DIM)), row((nh, 1)), row((1, IDX_DIM))] + pspecs,
            out_specs=[row((n_pages, PAGE_SIZE)), row((8, LANES))],
        ),
        out_shape=[jax.ShapeDtypeStruct((nb, n_pages, PAGE_SIZE), I32),
                   jax.ShapeDtypeStruct((nb, 8, LANES), I32)],
        compiler_params=_params(("parallel", "arbitrary")),
    )(page_table, qi.reshape(nb, nh, IDX_DIM), wi.reshape(nb, nh, 1), ki_new.reshape(nb, 1, IDX_DIM),
      *([cache_t] * pages_per_step))


def _dsa_sample_kernel(pt_ref, key_ref, meta_ref, q_ref, kn_ref, vn_ref, *rest, n_heads, n_kv, pages_per_step):
    del pt_ref
    k_refs = rest[:pages_per_step]
    v_refs = rest[pages_per_step:2 * pages_per_step]
    o_ref, m_ref, l_ref, acc_ref = rest[2 * pages_per_step:]
    j = pl.program_id(1)
    rep = n_heads // n_kv

    @pl.when(j == 0)
    def _():
        m_ref[...] = jnp.full(m_ref.shape, SOFTMAX_M_INIT, F32)
        l_ref[...] = jnp.zeros_like(l_ref)
        acc_ref[...] = jnp.zeros_like(acc_ref)

    q = q_ref[...]
    meta = meta_ref[...]
    thr = meta[0:1, :]
    group = lax.broadcasted_iota(I32, (n_heads, PAGE_SIZE), 0) // rep
    m, l, acc = m_ref[...], l_ref[...], acc_ref[...]

    def update(m, l, acc, s, pv):
        m_new = jnp.maximum(m, jnp.max(s, axis=1, keepdims=True))
        p = jnp.exp(s - m_new)
        alpha = jnp.exp(m - m_new)
        return m_new, alpha * l + jnp.sum(p, axis=1, keepdims=True), alpha * acc + pv(p)

    step_keys = pages_per_step * PAGE_SIZE
    group_all = lax.broadcasted_iota(I32, (n_heads, step_keys), 0) // rep

    def group_rows(refs, g):
        return jnp.concatenate([r[pl.ds(g, PAGE_SIZE, stride=n_kv), :].astype(BF16) for r in refs], axis=0)

    bias = jnp.concatenate(
        [jnp.where(key_ref[pl.ds(j * pages_per_step + p_i, 1), :] >= thr, 0.0, MASK_BIAS)
         for p_i in range(pages_per_step)], axis=1)
    s_all = jnp.zeros((n_heads, step_keys), F32)
    for g in range(n_kv):
        sg = lax.dot_general(q, group_rows(k_refs, g), _NT, preferred_element_type=F32)
        s_all = jnp.where(group_all == g, sg, s_all)

    def pv(p):
        pb = p.astype(BF16)
        out = jnp.zeros((n_heads, HEAD_DIM), F32)
        for g in range(n_kv):
            og = jnp.dot(pb, group_rows(v_refs, g), preferred_element_type=F32)
            out = jnp.where(group == g, og, out)
        return out

    m, l, acc = update(m, l, acc, s_all * SCALE + bias, pv)
    m_ref[...], l_ref[...], acc_ref[...] = m, l, acc

    @pl.when(j == pl.num_programs(1) - 1)
    def _():
        own_bias = jnp.where(meta[1:2, 0:1] > 0, 0.0, MASK_BIAS)
        s_own = jnp.sum(q.astype(F32) * kn_ref[...].astype(F32), axis=1, keepdims=True) * SCALE + own_bias
        vn = vn_ref[...].astype(F32)
        _, l_f, acc_f = update(m, l, acc, s_own, lambda p: p.astype(BF16).astype(F32) * vn)
        o_ref[...] = (acc_f / l_f).astype(BF16)


def _dsa_sample_call(page_table, keys, meta, qb, kb_new, vb_new, cache_k, cache_v, *, layer, pages_per_step):
    nb, n_pages = page_table.shape
    n_kv = cache_k.shape[3]
    n_heads = qb.shape[1] // HEAD_DIM
    rep = n_heads // n_kv
    row = lambda s: pl.BlockSpec((None,) + s, lambda b, j, pt: (b, 0, 0))
    own = lambda t: jnp.repeat(t.reshape(nb, n_kv, HEAD_DIM), rep, axis=1)
    page_block = (None, None, PAGE_SIZE * n_kv, HEAD_DIM)
    kspecs = _page_specs(page_block, n_pages, pages_per_step, layer, False)
    out = pl.pallas_call(
        functools.partial(_dsa_sample_kernel, n_heads=n_heads, n_kv=n_kv, pages_per_step=pages_per_step),
        grid_spec=pltpu.PrefetchScalarGridSpec(
            num_scalar_prefetch=1,
            grid=(nb, n_pages // pages_per_step),
            in_specs=[row((n_pages, PAGE_SIZE)), row((8, LANES)), row((n_heads, HEAD_DIM)),
                      row((n_heads, HEAD_DIM)), row((n_heads, HEAD_DIM))] + kspecs + kspecs,
            out_specs=row((n_heads, HEAD_DIM)),
            scratch_shapes=[pltpu.VMEM((n_heads, 1), F32), pltpu.VMEM((n_heads, 1), F32),
                            pltpu.VMEM((n_heads, HEAD_DIM), F32)],
        ),
        out_shape=jax.ShapeDtypeStruct((nb, n_heads, HEAD_DIM), BF16),
        compiler_params=_params(("parallel", "arbitrary")),
    )(page_table, keys, meta, qb.reshape(nb, n_heads, HEAD_DIM), own(kb_new), own(vb_new),
      *([_flat_pages(cache_k)] * pages_per_step), *([_flat_pages(cache_v)] * pages_per_step))
    return out.reshape(nb, n_heads * HEAD_DIM)


def _merge_kernel(x_ref, oa_ref, ob_ref, ga_ref, gb_ref, wa_ref, wb_ref, wo_ref, g_ref, y_ref, hn_ref):
    mixed = (ga_ref[...] * jnp.dot(oa_ref[...], wa_ref[...], preferred_element_type=F32)
             + gb_ref[...] * jnp.dot(ob_ref[...], wb_ref[...], preferred_element_type=F32))
    y = x_ref[...] + jnp.dot(mixed.astype(BF16), wo_ref[...], preferred_element_type=F32)
    y_ref[...] = y
    hn_ref[...] = _rms(y, g_ref[...]).astype(BF16)


def _merge_call(x, oa, ob, gates, wa, wb, wo, g_next, *, tm):
    n, d = x.shape
    row = lambda w: pl.BlockSpec((tm, w), lambda i: (i, 0))
    const = lambda s: pl.BlockSpec(s, lambda i: (0, 0), pipeline_mode=pl.Buffered(1))
    return pl.pallas_call(
        _merge_kernel,
        grid=(n // tm,),
        in_specs=[row(d), row(oa.shape[1]), row(ob.shape[1]),
                  pl.BlockSpec((tm, d), lambda i: (i, 0)), pl.BlockSpec((tm, d), lambda i: (i, 1)),
                  const(wa.shape), const(wb.shape), const(wo.shape), const((1, d))],
        out_specs=[row(d), row(d)],
        out_shape=[jax.ShapeDtypeStruct((n, d), F32), jax.ShapeDtypeStruct((n, d), BF16)],
        compiler_params=_params(("parallel",)),
    )(x, oa, ob, gates, gates, wa, wb, wo, g_next.reshape(1, d))


def _rope_tables(pos, head_dim):
    rot = head_dim // ROT_FRACTION
    half = rot // 2
    inv = 1.0 / (ROPE_THETA ** (jnp.arange(half, dtype=F32) * 2.0 / rot))
    ang = pos.astype(F32)[:, None] * inv[None, :]
    cos, sin = jnp.cos(ang), jnp.sin(ang)
    lane = np.arange(LANES) % head_dim
    first, second = lane < half, (lane >= half) & (lane < rot)
    freq = np.where(second, lane - half, np.where(first, lane, 0))
    cos_t = jnp.where(first | second, cos[:, freq], 1.0)
    sin_lo = jnp.where(first, -sin[:, freq], 0.0)
    sin_hi = jnp.where(second, sin[:, freq], 0.0)
    return cos_t, sin_lo, sin_hi


def _suffix_sum_rhs(n):
    strict_lower = (np.arange(n)[:, None] > np.arange(n)[None, :]).astype(np.float32)
    return np.concatenate([strict_lower, np.ones((n, n), np.float32)], axis=1)


def _row_tile(n, pref):
    return pref if n % pref == 0 else n


def _pages_per_step(n_pages, pref):
    while n_pages % pref:
        pref //= 2
    return pref


def kernel(x_prompt, x_sample, cache_sb_k, cache_sb_v, cache_dsa_k, cache_dsa_v, cache_idx_k, page_table, ffn1_g, ffn1_w1, ffn1_w3, ffn1_w2, mix_g, w_in, w_br_sb, w_br_dsa, w_out, ffn2_g, ffn2_w1, ffn2_w3, ffn2_w2, final_g):
    bsz, seq, d = x_prompt.shape
    nb, n_new, _ = x_sample.shape
    assert n_new == 1, "the sample path is written for one new token per sequence"
    depth = ffn1_g.shape[0]
    n_sb = cache_sb_k.shape[3]
    n_kv = cache_dsa_k.shape[3]
    w_sb = n_sb * HEAD_DIM
    w_dq = w_br_dsa.shape[1]
    n_dsa = w_dq // HEAD_DIM
    wkv = n_kv * HEAD_DIM
    n_idx = (w_in.shape[2] - 3 * w_sb - w_dq - 2 * wkv - IDX_DIM - 2 * d) // (IDX_DIM + 1)
    w_iq = n_idx * IDX_DIM
    assert w_sb == w_dq == w_iq and 2 * wkv + 2 * LANES <= w_sb
    n_pages = page_table.shape[1]
    past = n_pages * PAGE_SIZE
    topk_p = min(TOPK_MAX, seq // 4)
    topk_s = min(TOPK_MAX, (past + n_new) // 4)

    pos_p = jnp.arange(seq)
    pos_s = jnp.full((nb,), past, jnp.int32)
    tabs_p = _rope_tables(pos_p, HEAD_DIM) + _rope_tables(pos_p, IDX_DIM)
    tabs_s = _rope_tables(pos_s, HEAD_DIM) + _rope_tables(pos_s, IDX_DIM)
    tq_sb = _row_tile(seq, 128)
    sum_rhs_q = jnp.asarray(_suffix_sum_rhs(tq_sb), BF16)
    sum_rhs_page = jnp.asarray(_suffix_sum_rhs(PAGE_SIZE), BF16)

    xp = x_prompt.reshape(bsz * seq, d)
    xs = x_sample.reshape(nb, d)
    tm_p = _row_tile(bsz * seq, 512)
    tm_in = _row_tile(seq, 512)
    tm_merge = _row_tile(bsz * seq, 256)
    tf = _row_tile(ffn1_w1.shape[2], 512)
    st_p = [[] for _ in range(5)]
    st_s = [[] for _ in range(5)]

    hp = _rms_call(xp, ffn1_g[0], tm_p)
    hs = _rms_call(xs, ffn1_g[0], nb)
    for layer in range(depth):
        bf = lambda w: w[layer].astype(BF16)
        w1a, w3a, w2a = bf(ffn1_w1), bf(ffn1_w3), bf(ffn1_w2)
        w1b, w3b, w2b = bf(ffn2_w1), bf(ffn2_w3), bf(ffn2_w2)
        wl = w_in[layer]
        off = np.cumsum([0, w_sb, w_sb, w_sb, w_dq, wkv, wkv, w_iq, IDX_DIM, n_idx, d, d])
        col = lambda k: wl[:, off[k]:off[k + 1]]
        pad = lambda w: jnp.zeros((d, w), F32)
        misc = jnp.concatenate([col(4), col(5), col(7), pad(LANES - IDX_DIM), col(8), pad(LANES - n_idx),
                                pad(w_sb - 2 * wkv - 2 * LANES)], axis=1)
        w_tiles = jnp.stack([col(0), col(1), col(2), col(3), col(6), misc]).astype(BF16)
        w_gates = wl[:, off[9]:off[11]].astype(BF16)
        wa, wb, wo = bf(w_br_sb), bf(w_br_dsa), bf(w_out)

        xp, hp = _ffn_call(xp, hp, w1a, w3a, w2a, mix_g[layer], final=False, tm=tm_p, tf=tf)
        xs, hs = _ffn_call(xs, hs, w1a, w3a, w2a, mix_g[layer], final=False, tm=nb, tf=tf)

        (qa, ka, kab, va, vab, qb_t, qi_t, kb, kbb, vb, vb_t, ki, kib, wi_t) = _inproj_call(
            hp, w_tiles, tabs_p, tm=tm_in, n_tab_blocks=seq // tm_in, wkv=wkv, n_idx_heads=n_idx,
            query_on_lanes=True)
        gates = _gates_call(hp, w_gates, tm=tm_p, tn=_row_tile(2 * d, 1024))
        oa = _sb_prompt_call(qa, kab, vab, sum_rhs_q, bsz=bsz, seq=seq, n_heads=n_sb, tq=tq_sb)
        ob = _dsa_prompt_call(qi_t, wi_t, kib, qb_t, kbb, vb_t, bsz=bsz, seq=seq, n_idx_heads=n_idx,
                              n_heads=n_dsa, n_kv=n_kv, topk=topk_p, tq=_row_tile(seq, 128))
        xp, hp = _merge_call(xp, oa, ob, gates, wa, wb, wo, ffn2_g[layer], tm=tm_merge)
        for store, val, shape in zip(st_p, (ka, va, kb, vb, ki),
                                     ((n_sb, HEAD_DIM), (n_sb, HEAD_DIM), (n_kv, HEAD_DIM), (n_kv, HEAD_DIM), (IDX_DIM,))):
            store.append(val.reshape((bsz, seq) + shape))

        (qa, ka, kab, va, vab, qb, qi, kb, kbb, vb, vbb, ki, kib, wi) = _inproj_call(
            hs, w_tiles, tabs_s, tm=nb, n_tab_blocks=1, wkv=wkv, n_idx_heads=n_idx, query_on_lanes=False)
        gates = _gates_call(hs, w_gates, tm=nb, tn=_row_tile(2 * d, 1024))
        oa = _sb_sample_call(page_table, qa, cache_sb_k, cache_sb_v, sum_rhs_page, layer=layer)
        keys, meta = _idx_sample_call(page_table, qi, wi, kib, cache_idx_k, layer=layer,
                                      pages_per_step=_pages_per_step(n_pages, 32), topk=topk_s)
        ob = _dsa_sample_call(page_table, keys, meta, qb, kbb, vbb, cache_dsa_k, cache_dsa_v, layer=layer,
                              pages_per_step=_pages_per_step(n_pages, 16))
        xs, hs = _merge_call(xs, oa, ob, gates, wa, wb, wo, ffn2_g[layer], tm=nb)
        for store, val, shape in zip(st_s, (ka, va, kb, vb, ki),
                                     ((n_sb, HEAD_DIM), (n_sb, HEAD_DIM), (n_kv, HEAD_DIM), (n_kv, HEAD_DIM), (IDX_DIM,))):
            store.append(val.reshape((nb, n_new) + shape))

        last = layer == depth - 1
        g_next = final_g if last else ffn1_g[layer + 1]
        if last:
            xp = _ffn_call(xp, hp, w1b, w3b, w2b, g_next, final=True, tm=tm_p, tf=tf)
            xs = _ffn_call(xs, hs, w1b, w3b, w2b, g_next, final=True, tm=nb, tf=tf)
        else:
            xp, hp = _ffn_call(xp, hp, w1b, w3b, w2b, g_next, final=False, tm=tm_p, tf=tf)
            xs, hs = _ffn_call(xs, hs, w1b, w3b, w2b, g_next, final=False, tm=nb, tf=tf)

    return (xp.reshape(bsz, seq, d), xs.reshape(nb, n_new, d),
            jnp.stack(st_p[0]), jnp.stack(st_p[1]), jnp.stack(st_p[2]), jnp.stack(st_p[3]), jnp.stack(st_p[4]),
            jnp.stack(st_s[0]), jnp.stack(st_s[1]), jnp.stack(st_s[2]), jnp.stack(st_s[3]), jnp.stack(st_s[4]))
```

```python
import functools

import jax
import jax.numpy as jnp
import numpy as np
from jax import lax
from jax.experimental import pallas as pl
from jax.experimental.pallas import tpu as pltpu

HEAD_DIM = 128
IDX_DIM = 64
TOPK_MAX = 256
ROPE_THETA = 500000.0
ROT_FRACTION = 4
EPS = 1e-6
PAGE_SIZE = 128
SCALE = HEAD_DIM ** -0.5
LOG2_E = 1.4426950408889634

LANES = 128
VMEM_LIMIT = 56 * 1024 * 1024

INT_MIN = -(2 ** 31)
NEG_KEY = -2139095041
SOFTMAX_M_INIT = -1e30
MASK_BIAS = -2e30
SB_LOG_UNDERFLOW = -110.0
DSA_KEY_BLOCK = 512

F32 = jnp.float32
BF16 = jnp.bfloat16
I32 = jnp.int32

_NT = (((1,), (1,)), ((), ()))


def _params(sem):
    return pltpu.CompilerParams(dimension_semantics=sem, vmem_limit_bytes=VMEM_LIMIT)


def _rms(x, g):
    return x * lax.rsqrt(jnp.mean(x * x, axis=-1, keepdims=True) + EPS) * g


def _log_sigmoid(z):
    return jnp.minimum(z, 0.0) - jnp.log(1.0 + jnp.exp(-jnp.abs(z)))


def _order_key(x):
    bits = pltpu.bitcast(x, I32)
    bits = jnp.where(bits == INT_MIN, 0, bits)
    return jnp.where(bits < 0, bits ^ 0x7FFFFFFF, bits)


def _split_bf16(x):
    hi = x.astype(BF16)
    lo = (x - hi.astype(F32)).astype(BF16)
    return hi, lo


def _rms_kernel(x_ref, g_ref, h_ref):
    h_ref[...] = _rms(x_ref[...], g_ref[...]).astype(BF16)


def _rms_call(x, g, tm):
    n, d = x.shape
    return pl.pallas_call(
        _rms_kernel,
        grid=(n // tm,),
        in_specs=[pl.BlockSpec((tm, d), lambda i: (i, 0)), pl.BlockSpec((1, d), lambda i: (0, 0))],
        out_specs=pl.BlockSpec((tm, d), lambda i: (i, 0)),
        out_shape=jax.ShapeDtypeStruct((n, d), BF16),
        compiler_params=_params(("parallel",)),
    )(x, g.reshape(1, d))


def _ffn_kernel(x_ref, h_ref, w1_ref, w3_ref, w2_ref, g_ref, *rest, final):
    if final:
        y_ref, acc_ref = rest
    else:
        y_ref, hn_ref, acc_ref = rest
    f = pl.program_id(1)

    @pl.when(f == 0)
    def _():
        acc_ref[...] = jnp.zeros_like(acc_ref)

    h = h_ref[...]
    a = jnp.dot(h, w1_ref[...], preferred_element_type=F32)
    b = jnp.dot(h, w3_ref[...], preferred_element_type=F32)
    u = (a * jax.nn.sigmoid(a)) * b
    acc_ref[...] += jnp.dot(u.astype(BF16), w2_ref[...], preferred_element_type=F32)

    @pl.when(f == pl.num_programs(1) - 1)
    def _():
        y = x_ref[...] + 0.5 * acc_ref[...]
        if final:
            y_ref[...] = _rms(y, g_ref[...])
        else:
            y_ref[...] = y
            hn_ref[...] = _rms(y, g_ref[...]).astype(BF16)


def _ffn_call(x, h, w1, w3, w2, g_next, *, final, tm, tf):
    n, d = x.shape
    dff = w1.shape[1]
    out_shape = [jax.ShapeDtypeStruct((n, d), F32)]
    out_specs = [pl.BlockSpec((tm, d), lambda i, f: (i, 0))]
    if not final:
        out_shape.append(jax.ShapeDtypeStruct((n, d), BF16))
        out_specs.append(pl.BlockSpec((tm, d), lambda i, f: (i, 0)))
    res = pl.pallas_call(
        functools.partial(_ffn_kernel, final=final),
        grid=(n // tm, dff // tf),
        in_specs=[
            pl.BlockSpec((tm, d), lambda i, f: (i, 0)),
            pl.BlockSpec((tm, d), lambda i, f: (i, 0)),
            pl.BlockSpec((d, tf), lambda i, f: (0, f)),
            pl.BlockSpec((d, tf), lambda i, f: (0, f)),
            pl.BlockSpec((tf, d), lambda i, f: (f, 0)),
            pl.BlockSpec((1, d), lambda i, f: (0, 0)),
        ],
        out_specs=out_specs,
        out_shape=out_shape,
        scratch_shapes=[pltpu.VMEM((tm, d), F32)],
        compiler_params=_params(("parallel", "arbitrary")),
    )(x, h, w1, w3, w2, g_next.reshape(1, d))
    return res[0] if final else res


def _rope(x, cos, sin_lo, sin_hi, half):
    return (x * cos + pltpu.roll(x, LANES - half, axis=1) * sin_lo
            + pltpu.roll(x, half, axis=1) * sin_hi)


def _inproj_kernel(h_ref, w_ref, cd_ref, sdl_ref, sdh_ref, ci_ref, sil_ref, sih_ref,
                   qa_ref, ka_ref, kab_ref, va_ref, vab_ref, qb_ref, qi_ref,
                   kb_ref, kbb_ref, vb_ref, vbb_ref, ki_ref, kib_ref, wi_ref, *, wkv, query_on_lanes):
    j = pl.program_id(1)
    acc = jnp.dot(h_ref[...], w_ref[0], preferred_element_type=F32)
    lay = (lambda x: x.T) if query_on_lanes else (lambda x: x)
    rot_d = HEAD_DIM // ROT_FRACTION // 2
    rot_i = IDX_DIM // ROT_FRACTION // 2

    def rope_cols(x, idx):
        if idx:
            c, sl, sh, half = ci_ref[...], sil_ref[...], sih_ref[...], rot_i
        else:
            c, sl, sh, half = cd_ref[...], sdl_ref[...], sdh_ref[...], rot_d
        return jnp.concatenate(
            [_rope(x[:, s:s + LANES], c, sl, sh, half) for s in range(0, x.shape[1], LANES)], axis=1)

    @pl.when(j == 0)
    def _():
        qa_ref[...] = acc.astype(BF16)

    @pl.when(j == 1)
    def _():
        ka_ref[...] = acc
        kab_ref[...] = acc.astype(BF16)

    @pl.when(j == 2)
    def _():
        va_ref[...] = acc
        vab_ref[...] = acc.astype(BF16)

    @pl.when(j == 3)
    def _():
        qb_ref[...] = lay(rope_cols(acc, False)).astype(BF16)

    @pl.when(j == 4)
    def _():
        qi_ref[...] = lay(rope_cols(acc, True)).astype(BF16)

    @pl.when(j == 5)
    def _():
        kb = rope_cols(acc[:, :wkv], False)
        kb_ref[...] = kb
        kbb_ref[...] = kb.astype(BF16)
        vb = acc[:, wkv:2 * wkv]
        vb_ref[...] = vb
        ki = rope_cols(acc[:, 2 * wkv:2 * wkv + LANES], True)[:, :IDX_DIM]
        ki_ref[...] = ki
        kib_ref[...] = ki.astype(BF16)
        wi_lanes = acc[:, 2 * wkv + LANES:2 * wkv + 2 * LANES]
        if query_on_lanes:
            vb_t = vb.T.astype(BF16)
            for c in range(vbb_ref.shape[0]):
                vbb_ref[c] = vb_t[:, c * vbb_ref.shape[2]:(c + 1) * vbb_ref.shape[2]]
            wi_ref[...] = wi_lanes.T[:wi_ref.shape[0], :]
        else:
            vbb_ref[...] = vb.astype(BF16)
            wi_ref[...] = wi_lanes[:, :wi_ref.shape[1]]


def _inproj_call(h, w_tiles, tabs, *, tm, n_tab_blocks, wkv, n_idx_heads, query_on_lanes):
    n, d = h.shape
    nt, _, tw = w_tiles.shape
    row = lambda i, j: (i, 0)
    tab = lambda i, j: (i % n_tab_blocks, 0)
    rows_major = lambda w, dt: (jax.ShapeDtypeStruct((n, w), dt), pl.BlockSpec((tm, w), row))
    if query_on_lanes:
        lanes_major = lambda w, dt: (jax.ShapeDtypeStruct((w, n), dt), pl.BlockSpec((w, tm), lambda i, j: (0, i)))
        v_blocks = (jax.ShapeDtypeStruct((n // DSA_KEY_BLOCK, wkv, DSA_KEY_BLOCK), BF16),
                    pl.BlockSpec((tm // DSA_KEY_BLOCK, wkv, DSA_KEY_BLOCK), lambda i, j: (i, 0, 0)))
    else:
        lanes_major = rows_major
        v_blocks = rows_major(wkv, BF16)
    outs = [rows_major(tw, BF16), rows_major(tw, F32), rows_major(tw, BF16), rows_major(tw, F32),
            rows_major(tw, BF16), lanes_major(tw, BF16), lanes_major(tw, BF16),
            rows_major(wkv, F32), rows_major(wkv, BF16), rows_major(wkv, F32), v_blocks,
            rows_major(IDX_DIM, F32), rows_major(IDX_DIM, BF16), lanes_major(n_idx_heads, F32)]
    return pl.pallas_call(
        functools.partial(_inproj_kernel, wkv=wkv, query_on_lanes=query_on_lanes),
        grid=(n // tm, nt),
        in_specs=[pl.BlockSpec((tm, d), row), pl.BlockSpec((1, d, tw), lambda i, j: (j, 0, 0))]
        + [pl.BlockSpec((tm, LANES), tab)] * 6,
        out_specs=[spec for _, spec in outs],
        out_shape=[shape for shape, _ in outs],
        compiler_params=_params(("parallel", "arbitrary")),
    )(h, w_tiles, *tabs)


def _gates_kernel(h_ref, w_ref, o_ref):
    o_ref[...] = jax.nn.sigmoid(jnp.dot(h_ref[...], w_ref[...], preferred_element_type=F32))


def _gates_call(h, wg, *, tm, tn):
    n, d = h.shape
    ncol = wg.shape[1]
    return pl.pallas_call(
        _gates_kernel,
        grid=(ncol // tn, n // tm),
        in_specs=[pl.BlockSpec((tm, d), lambda j, i: (i, 0)), pl.BlockSpec((d, tn), lambda j, i: (0, j))],
        out_specs=pl.BlockSpec((tm, tn), lambda j, i: (i, j)),
        out_shape=jax.ShapeDtypeStruct((n, ncol), F32),
        compiler_params=_params(("parallel", "arbitrary")),
    )(h, wg)


def _sb_weights(z, sum_rhs, carry, mask):
    n = z.shape[1]
    ls = _log_sigmoid(z)
    lr = ls - z
    if mask is not None:
        lr = jnp.where(mask, lr, 0.0)
    hi, lo = _split_bf16(lr)
    sums = (jnp.dot(hi, sum_rhs, preferred_element_type=F32)
            + jnp.dot(lo, sum_rhs, preferred_element_type=F32))
    a = jnp.exp(ls + sums[:, :n] + carry)
    if mask is not None:
        a = jnp.where(mask, a, 0.0)
    return a, carry + sums[:, n:]


def _sb_prompt_kernel(q_ref, k_ref, v_ref, rhs_ref, o_ref, carry_ref, acc_ref, *, tq, n_heads):
    qi = pl.program_id(1)
    rhs = rhs_ref[...]
    rows = lax.broadcasted_iota(I32, (n_heads * tq, tq), 0) % tq
    cols = lax.broadcasted_iota(I32, (n_heads * tq, tq), 1)
    head_lanes = [slice(h * HEAD_DIM, (h + 1) * HEAD_DIM) for h in range(n_heads)]

    def block(kb, mask, first):
        ks = pl.multiple_of(kb * tq, tq)
        z = jnp.concatenate(
            [lax.dot_general(q_ref[:, hl], k_ref[pl.ds(ks, tq), hl], _NT, preferred_element_type=F32)
             for hl in head_lanes], axis=0) * SCALE
        a, carry = _sb_weights(z, rhs, 0.0 if first else carry_ref[...], mask)
        carry_ref[...] = carry
        ab = a.astype(BF16)
        pv = jnp.concatenate(
            [jnp.dot(ab[h * tq:(h + 1) * tq], v_ref[pl.ds(ks, tq), hl], preferred_element_type=F32)
             for h, hl in enumerate(head_lanes)], axis=0)
        acc_ref[...] = pv if first else acc_ref[...] + pv
        return jnp.max(carry)

    def more(state):
        return (state[0] >= 0) & (state[1] > SB_LOG_UNDERFLOW)

    def step(state):
        return state[0] - 1, block(state[0], None, False)

    lax.while_loop(more, step, (qi - 1, block(qi, cols < rows, True)))
    for h, hl in enumerate(head_lanes):
        o_ref[:, hl] = acc_ref[h * tq:(h + 1) * tq, :].astype(BF16)


def _sb_prompt_call(qa, ka, va, sum_rhs, *, bsz, seq, n_heads, tq):
    width = n_heads * HEAD_DIM
    q3 = qa.reshape(bsz, seq, width)
    k3 = ka.reshape(bsz, seq, width)
    v3 = va.reshape(bsz, seq, width)
    qspec = pl.BlockSpec((None, tq, width), lambda b, i: (b, i, 0))
    kspec = pl.BlockSpec((None, seq, width), lambda b, i: (b, 0, 0), pipeline_mode=pl.Buffered(1))
    out = pl.pallas_call(
        functools.partial(_sb_prompt_kernel, tq=tq, n_heads=n_heads),
        grid=(bsz, seq // tq),
        in_specs=[qspec, kspec, kspec, pl.BlockSpec((tq, 2 * tq), lambda b, i: (0, 0))],
        out_specs=qspec,
        out_shape=jax.ShapeDtypeStruct((bsz, seq, width), BF16),
        scratch_shapes=[pltpu.VMEM((n_heads * tq, tq), F32), pltpu.VMEM((n_heads * tq, HEAD_DIM), F32)],
        compiler_params=_params(("parallel", "arbitrary")),
    )(q3, k3, v3, sum_rhs)
    return out.reshape(bsz * seq, width)


def _kth_largest_key(count_ge, shape, topk):
    def body(it, pat):
        cand = pat ^ jnp.left_shift(jnp.int32(1), 31 - it)
        return jnp.where(count_ge(cand) >= topk, cand, pat)

    return lax.fori_loop(0, 32, body, jnp.full(shape, INT_MIN, I32))


def _tie_cut(count_eq_before, need, shape, nbits):
    def body(it, c):
        cand = c | jnp.left_shift(jnp.int32(1), nbits - 1 - it)
        return jnp.where(count_eq_before(cand) < need, cand, c)

    return lax.fori_loop(0, nbits, body, jnp.zeros(shape, I32)) + 1


def _dsa_prompt_kernel(qi_ref, wi_ref, ki_ref, qb_ref, kb_ref, vb_ref, o_ref,
                       key_ref, m_ref, l_ref, acc_ref, *, tq, tk, topk, n_idx_heads, n_kv, rep, pos_bits):
    i = pl.program_id(1)
    q0 = i * tq
    nkb = (q0 + tq + tk - 1) // tk
    kpos = lax.broadcasted_iota(I32, (tk, tq), 0)
    qpos = q0 + lax.broadcasted_iota(I32, (tk, tq), 1)

    pair_rhs = [jnp.concatenate([qi_ref[h * IDX_DIM:(h + 1) * IDX_DIM, :],
                                 qi_ref[(h + 1) * IDX_DIM:(h + 2) * IDX_DIM, :]], axis=1)
                for h in range(0, n_idx_heads, 2)]
    w_row = [wi_ref[h:h + 1, :] for h in range(n_idx_heads)]

    def score_block(kb, _):
        ks = pl.multiple_of(kb * tk, tk)
        kblk = ki_ref[pl.ds(ks, tk), :]
        sc = jnp.zeros((tk, tq), F32)
        for p, rhs in enumerate(pair_rhs):
            logit = jnp.dot(kblk, rhs, preferred_element_type=F32)
            sc = (sc + jnp.maximum(logit[:, :tq], 0.0) * w_row[2 * p]
                  + jnp.maximum(logit[:, tq:], 0.0) * w_row[2 * p + 1])
        sc = jnp.where(ks + kpos <= qpos, sc, -jnp.inf)
        key_ref[pl.ds(ks, tk), :] = _order_key(sc)
        return 0

    lax.fori_loop(0, nkb, score_block, 0)

    def count(pred):
        def body(kb, cnt):
            ks = pl.multiple_of(kb * tk, tk)
            hit = pred(key_ref[pl.ds(ks, tk), :], ks + kpos).astype(I32)
            return cnt + jnp.sum(hit.reshape(tk // 8, 8, tq), axis=0)

        cnt = lax.fori_loop(0, nkb, body, jnp.zeros((8, tq), I32))
        return jnp.sum(cnt, axis=0, keepdims=True)

    count_ge = lambda v: count(lambda key, pos: key >= v)
    kth = _kth_largest_key(count_ge, (1, tq), topk)
    thr = jnp.maximum(kth, NEG_KEY + 1)

    n_ge = count_ge(thr)

    @pl.when(jnp.max(n_ge) > topk)
    def _():
        need = topk - count(lambda key, pos: key > thr)
        cut = _tie_cut(lambda c: count(lambda key, pos: (key == thr) & (pos < c)), need, (1, tq), pos_bits)

        def drop(kb, _):
            ks = pl.multiple_of(kb * tk, tk)
            key = key_ref[pl.ds(ks, tk), :]
            key_ref[pl.ds(ks, tk), :] = jnp.where((key == thr) & (ks + kpos >= cut), NEG_KEY, key)
            return 0

        lax.fori_loop(0, nkb, drop, 0)

    m_ref[...] = jnp.full(m_ref.shape, SOFTMAX_M_INIT, F32)
    l_ref[...] = jnp.zeros_like(l_ref)
    acc_ref[...] = jnp.zeros_like(acc_ref)
    q_cols = [jnp.concatenate([qb_ref[(g * rep + r) * HEAD_DIM:(g * rep + r + 1) * HEAD_DIM, :]
                               for r in range(rep)], axis=1) for g in range(n_kv)]

    def attend(kb, _):
        ks = pl.multiple_of(kb * tk, tk)
        bias = jnp.where(key_ref[pl.ds(ks, tk), :] >= thr, 0.0, MASK_BIAS)
        bias = jnp.concatenate([bias] * rep, axis=1)
        s = [jnp.dot(kb_ref[pl.ds(ks, tk), g * HEAD_DIM:(g + 1) * HEAD_DIM], q_cols[g],
                     preferred_element_type=F32) * (SCALE * LOG2_E) + bias for g in range(n_kv)]
        m_old = [m_ref[g] for g in range(n_kv)]
        m_new = [jnp.maximum(m_old[g], jnp.max(s[g], axis=0, keepdims=True)) for g in range(n_kv)]
        p = [jnp.exp2(s[g] - m_new[g]) for g in range(n_kv)]
        for g in range(n_kv):
            alpha = jnp.exp2(m_old[g] - m_new[g])
            l_ref[g] = alpha * l_ref[g] + jnp.sum(p[g], axis=0, keepdims=True)
            v_t = vb_ref[kb, g * HEAD_DIM:(g + 1) * HEAD_DIM, :]
            acc_ref[g] = alpha * acc_ref[g] + jnp.dot(v_t, p[g].astype(BF16), preferred_element_type=F32)
            m_ref[g] = m_new[g]
        return 0

    lax.fori_loop(0, nkb, attend, 0)
    for g in range(n_kv):
        out_t = acc_ref[g] / l_ref[g]
        for r in range(rep):
            h = g * rep + r
            o_ref[:, h * HEAD_DIM:(h + 1) * HEAD_DIM] = out_t[:, r * tq:(r + 1) * tq].T.astype(BF16)


def _dsa_prompt_call(qi_t, wi_t, ki, qb_t, kb, vb_t, *, bsz, seq, n_idx_heads, n_heads, n_kv, topk, tq):
    rep = n_heads // n_kv
    nq = seq // tq
    tk = vb_t.shape[2]
    wkv = n_kv * HEAD_DIM
    wq, wb = n_idx_heads * IDX_DIM, n_heads * HEAD_DIM
    qspec = lambda w: pl.BlockSpec((w, tq), lambda b, i: (0, b * nq + i))
    kspec = lambda w: pl.BlockSpec((None, seq, w), lambda b, i: (b, 0, 0))
    out = pl.pallas_call(
        functools.partial(_dsa_prompt_kernel, tq=tq, tk=tk, topk=topk, n_idx_heads=n_idx_heads,
                          n_kv=n_kv, rep=rep, pos_bits=max(1, int(seq - 1).bit_length())),
        grid=(bsz, nq),
        in_specs=[qspec(wq), qspec(n_idx_heads), kspec(IDX_DIM), qspec(wb), kspec(wkv),
                  pl.BlockSpec((seq // tk, wkv, tk), lambda b, i: (b, 0, 0))],
        out_specs=pl.BlockSpec((None, tq, wb), lambda b, i: (b, i, 0)),
        out_shape=jax.ShapeDtypeStruct((bsz, seq, wb), BF16),
        scratch_shapes=[pltpu.VMEM((seq, tq), I32),
                        pltpu.VMEM((n_kv, 1, rep * tq), F32),
                        pltpu.VMEM((n_kv, 1, rep * tq), F32),
                        pltpu.VMEM((n_kv, HEAD_DIM, rep * tq), F32)],
        compiler_params=_params(("parallel", "arbitrary")),
    )(qi_t, wi_t, ki.reshape(bsz, seq, IDX_DIM), qb_t, kb.reshape(bsz, seq, wkv), vb_t)
    return out.reshape(bsz * seq, wb)


def _sb_sample_kernel(pt_ref, q_ref, rhs_ref, ck_ref, cv_ref, o_ref, kbuf, vbuf, sem, *, layer, n_heads, n_pages):
    b = pl.program_id(0)
    q = q_ref[...]
    rhs = rhs_ref[...]
    head = lax.broadcasted_iota(I32, (n_heads, PAGE_SIZE), 0)

    def page_copies(n, slot):
        page = pt_ref[b, n_pages - 1 - n]
        return (pltpu.make_async_copy(ck_ref.at[layer, page], kbuf.at[slot], sem.at[0, slot]),
                pltpu.make_async_copy(cv_ref.at[layer, page], vbuf.at[slot], sem.at[1, slot]))

    def start(n, slot):
        for copy in page_copies(n, slot):
            copy.start()

    def wait(n, slot):
        for copy in page_copies(n, slot):
            copy.wait()

    def more(state):
        return (state[0] < n_pages) & (state[1] > SB_LOG_UNDERFLOW)

    def step(state):
        n, _, carry, acc = state
        slot = n % 2
        wait(n, slot)

        @pl.when(n + 1 < n_pages)
        def _():
            start(n + 1, 1 - slot)

        z = jnp.zeros((n_heads, PAGE_SIZE), F32)
        for h in range(n_heads):
            kh = kbuf[slot, pl.ds(h, PAGE_SIZE, stride=n_heads), :].astype(BF16)
            z = jnp.where(head == h, lax.dot_general(q, kh, _NT, preferred_element_type=F32), z)
        a, carry = _sb_weights(z * SCALE, rhs, carry, None)
        ab = a.astype(BF16)
        for h in range(n_heads):
            vh = vbuf[slot, pl.ds(h, PAGE_SIZE, stride=n_heads), :].astype(BF16)
            acc = acc + jnp.where(head == h, jnp.dot(ab, vh, preferred_element_type=F32), 0.0)
        return n + 1, jnp.max(carry), carry, acc

    start(0, 0)
    n_done, _, _, acc = lax.while_loop(
        more, step, (jnp.int32(0), jnp.float32(0.0), jnp.zeros((n_heads, PAGE_SIZE), F32),
                     jnp.zeros((n_heads, HEAD_DIM), F32)))

    @pl.when(n_done < n_pages)
    def _():
        wait(n_done, n_done % 2)

    o_ref[...] = acc.astype(BF16)


def _page_specs(block, n_steps_pages, pages_per_step, layer, reverse):
    specs = []
    for p in range(pages_per_step):
        if reverse:
            pick = lambda b, j, pt, p=p: pt[b, n_steps_pages - 1 - (j * pages_per_step + p)]
        else:
            pick = lambda b, j, pt, p=p: pt[b, j * pages_per_step + p]
        zeros = (0,) * (len(block) - 2)
        specs.append(pl.BlockSpec(block, lambda b, j, pt, pick=pick: (layer, pick(b, j, pt)) + zeros))
    return specs


def _flat_pages(cache):
    depth, pool, page, heads, dim = cache.shape
    return cache.reshape(depth, pool, page * heads, dim)


def _sb_sample_call(page_table, qa, cache_k, cache_v, sum_rhs, *, layer):
    nb, n_pages = page_table.shape
    n_heads = cache_k.shape[3]
    q3 = qa.reshape(nb, n_heads, HEAD_DIM)
    row = pl.BlockSpec((None, n_heads, HEAD_DIM), lambda b, pt: (b, 0, 0))
    page_rows = PAGE_SIZE * n_heads
    out = pl.pallas_call(
        functools.partial(_sb_sample_kernel, layer=layer, n_heads=n_heads, n_pages=n_pages),
        grid_spec=pltpu.PrefetchScalarGridSpec(
            num_scalar_prefetch=1,
            grid=(nb,),
            in_specs=[row, pl.BlockSpec((PAGE_SIZE, 2 * PAGE_SIZE), lambda b, pt: (0, 0)),
                      pl.BlockSpec(memory_space=pl.ANY), pl.BlockSpec(memory_space=pl.ANY)],
            out_specs=row,
            scratch_shapes=[pltpu.VMEM((2, page_rows, HEAD_DIM), F32), pltpu.VMEM((2, page_rows, HEAD_DIM), F32),
                            pltpu.SemaphoreType.DMA((2, 2))],
        ),
        out_shape=jax.ShapeDtypeStruct((nb, n_heads, HEAD_DIM), BF16),
        compiler_params=_params(("arbitrary",)),
    )(page_table, q3, sum_rhs, _flat_pages(cache_k), _flat_pages(cache_v))
    return out.reshape(nb, n_heads * HEAD_DIM)


def _idx_sample_kernel(pt_ref, qi_ref, wi_ref, kin_ref, *rest, pages_per_step, topk, pos_bits):
    del pt_ref
    page_refs = rest[:pages_per_step]
    key_ref, meta_ref = rest[pages_per_step:]
    j = pl.program_id(1)
    qi = qi_ref[...]
    wi = wi_ref[...]
    logit = jnp.concatenate(
        [jnp.dot(qi, page_ref[...].astype(BF16), preferred_element_type=F32)
         for page_ref in page_refs], axis=1)
    keys = _order_key(jnp.sum(jnp.maximum(logit, 0.0) * wi, axis=0, keepdims=True))
    for p in range(pages_per_step):
        key_ref[pl.ds(j * pages_per_step + p, 1), :] = keys[:, p * PAGE_SIZE:(p + 1) * PAGE_SIZE]

    @pl.when(j == pl.num_programs(1) - 1)
    def _():
        n_pages = key_ref.shape[0]
        past = n_pages * PAGE_SIZE
        pos = (lax.broadcasted_iota(I32, (n_pages, PAGE_SIZE), 0) * PAGE_SIZE
               + lax.broadcasted_iota(I32, (n_pages, PAGE_SIZE), 1))
        own = jnp.sum(qi.astype(F32) * kin_ref[...].astype(F32), axis=1, keepdims=True)
        own_key = _order_key(jnp.sum(jnp.maximum(own, 0.0) * wi, axis=0, keepdims=True))

        def total(x):
            return jnp.sum(jnp.sum(x.astype(I32), axis=1, keepdims=True), axis=0, keepdims=True)

        def count(pred):
            return total(pred(key_ref[...], pos)) + pred(own_key, past).astype(I32)

        count_ge = lambda v: count(lambda key, where: key >= v)
        thr = _kth_largest_key(count_ge, (1, 1), topk)
        need = topk - count(lambda key, where: key > thr)
        cut = _tie_cut(lambda c: count(lambda key, where: (key == thr) & (where < c)), need, (1, 1), pos_bits)
        keys = key_ref[...]
        key_ref[...] = jnp.where((keys == thr) & (pos >= cut), NEG_KEY, keys)
        own_sel = (own_key > thr) | ((own_key == thr) & (past < cut))
        rowid = lax.broadcasted_iota(I32, meta_ref.shape, 0)
        meta_ref[...] = jnp.where(rowid == 0, thr, own_sel.astype(I32))


def _idx_sample_call(page_table, qi, wi, ki_new, cache_idx, *, layer, pages_per_step, topk):
    nb, n_pages = page_table.shape
    nh = wi.shape[1]
    row = lambda s: pl.BlockSpec((None,) + s, lambda b, j, pt: (b, 0, 0))
    cache_t = jnp.swapaxes(cache_idx, 2, 3)
    pspecs = _page_specs((None, None, IDX_DIM, PAGE_SIZE), n_pages, pages_per_step, layer, False)
    return pl.pallas_call(
        functools.partial(_idx_sample_kernel, pages_per_step=pages_per_step, topk=topk,
                          pos_bits=int(n_pages * PAGE_SIZE).bit_length()),
        grid_spec=pltpu.PrefetchScalarGridSpec(
            num_scalar_prefetch=1,
            grid=(nb, n_pages // pages_per_step),
            in_specs=[row((nh, IDX_DIM)), row((nh, 1)), row((1, IDX_DIM))] + pspecs,
            out_specs=[row((n_pages, PAGE_SIZE)), row((8, LANES))],
        ),
        out_shape=[jax.ShapeDtypeStruct((nb, n_pages, PAGE_SIZE), I32),
                   jax.ShapeDtypeStruct((nb, 8, LANES), I32)],
        compiler_params=_params(("parallel", "arbitrary")),
    )(page_table, qi.reshape(nb, nh, IDX_DIM), wi.reshape(nb, nh, 1), ki_new.reshape(nb, 1, IDX_DIM),
      *([cache_t] * pages_per_step))


def _dsa_sample_kernel(pt_ref, key_ref, meta_ref, q_ref, kn_ref, vn_ref, *rest, n_heads, n_kv, pages_per_step):
    del pt_ref
    k_refs = rest[:pages_per_step]
    v_refs = rest[pages_per_step:2 * pages_per_step]
    o_ref, m_ref, l_ref, acc_ref = rest[2 * pages_per_step:]
    j = pl.program_id(1)
    rep = n_heads // n_kv

    @pl.when(j == 0)
    def _():
        m_ref[...] = jnp.full(m_ref.shape, SOFTMAX_M_INIT, F32)
        l_ref[...] = jnp.zeros_like(l_ref)
        acc_ref[...] = jnp.zeros_like(acc_ref)

    q = q_ref[...]
    meta = meta_ref[...]
    thr = meta[0:1, :]
    group = lax.broadcasted_iota(I32, (n_heads, PAGE_SIZE), 0) // rep
    m, l, acc = m_ref[...], l_ref[...], acc_ref[...]

    def update(m, l, acc, s, pv):
        m_new = jnp.maximum(m, jnp.max(s, axis=1, keepdims=True))
        p = jnp.exp(s - m_new)
        alpha = jnp.exp(m - m_new)
        return m_new, alpha * l + jnp.sum(p, axis=1, keepdims=True), alpha * acc + pv(p)

    step_keys = pages_per_step * PAGE_SIZE
    group_all = lax.broadcasted_iota(I32, (n_heads, step_keys), 0) // rep

    def group_rows(refs, g):
        return jnp.concatenate([r[pl.ds(g, PAGE_SIZE, stride=n_kv), :].astype(BF16) for r in refs], axis=0)

    bias = jnp.concatenate(
        [jnp.where(key_ref[pl.ds(j * pages_per_step + p_i, 1), :] >= thr, 0.0, MASK_BIAS)
         for p_i in range(pages_per_step)], axis=1)
    s_all = jnp.zeros((n_heads, step_keys), F32)
    for g in range(n_kv):
        sg = lax.dot_general(q, group_rows(k_refs, g), _NT, preferred_element_type=F32)
        s_all = jnp.where(group_all == g, sg, s_all)

    def pv(p):
        pb = p.astype(BF16)
        out = jnp.zeros((n_heads, HEAD_DIM), F32)
        for g in range(n_kv):
            og = jnp.dot(pb, group_rows(v_refs, g), preferred_element_type=F32)
            out = jnp.where(group == g, og, out)
        return out

    m, l, acc = update(m, l, acc, s_all * SCALE + bias, pv)
    m_ref[...], l_ref[...], acc_ref[...] = m, l, acc

    @pl.when(j == pl.num_programs(1) - 1)
    def _():
        own_bias = jnp.where(meta[1:2, 0:1] > 0, 0.0, MASK_BIAS)
        s_own = jnp.sum(q.astype(F32) * kn_ref[...].astype(F32), axis=1, keepdims=True) * SCALE + own_bias
        vn = vn_ref[...].astype(F32)
        _, l_f, acc_f = update(m, l, acc, s_own, lambda p: p.astype(BF16).astype(F32) * vn)
        o_ref[...] = (acc_f / l_f).astype(BF16)


def _dsa_sample_call(page_table, keys, meta, qb, kb_new, vb_new, cache_k, cache_v, *, layer, pages_per_step):
    nb, n_pages = page_table.shape
    n_kv = cache_k.shape[3]
    n_heads = qb.shape[1] // HEAD_DIM
    rep = n_heads // n_kv
    row = lambda s: pl.BlockSpec((None,) + s, lambda b, j, pt: (b, 0, 0))
    own = lambda t: jnp.repeat(t.reshape(nb, n_kv, HEAD_DIM), rep, axis=1)
    page_block = (None, None, PAGE_SIZE * n_kv, HEAD_DIM)
    kspecs = _page_specs(page_block, n_pages, pages_per_step, layer, False)
    out = pl.pallas_call(
        functools.partial(_dsa_sample_kernel, n_heads=n_heads, n_kv=n_kv, pages_per_step=pages_per_step),
        grid_spec=pltpu.PrefetchScalarGridSpec(
            num_scalar_prefetch=1,
            grid=(nb, n_pages // pages_per_step),
            in_specs=[row((n_pages, PAGE_SIZE)), row((8, LANES)), row((n_heads, HEAD_DIM)),
                      row((n_heads, HEAD_DIM)), row((n_heads, HEAD_DIM))] + kspecs + kspecs,
            out_specs=row((n_heads, HEAD_DIM)),
            scratch_shapes=[pltpu.VMEM((n_heads, 1), F32), pltpu.VMEM((n_heads, 1), F32),
                            pltpu.VMEM((n_heads, HEAD_DIM), F32)],
        ),
        out_shape=jax.ShapeDtypeStruct((nb, n_heads, HEAD_DIM), BF16),
        compiler_params=_params(("parallel", "arbitrary")),
    )(page_table, keys, meta, qb.reshape(nb, n_heads, HEAD_DIM), own(kb_new), own(vb_new),
      *([_flat_pages(cache_k)] * pages_per_step), *([_flat_pages(cache_v)] * pages_per_step))
    return out.reshape(nb, n_heads * HEAD_DIM)


def _merge_kernel(x_ref, oa_ref, ob_ref, ga_ref, gb_ref, wa_ref, wb_ref, wo_ref, g_ref, y_ref, hn_ref):
    mixed = (ga_ref[...] * jnp.dot(oa_ref[...], wa_ref[...], preferred_element_type=F32)
             + gb_ref[...] * jnp.dot(ob_ref[...], wb_ref[...], preferred_element_type=F32))
    y = x_ref[...] + jnp.dot(mixed.astype(BF16), wo_ref[...], preferred_element_type=F32)
    y_ref[...] = y
    hn_ref[...] = _rms(y, g_ref[...]).astype(BF16)


def _merge_call(x, oa, ob, gates, wa, wb, wo, g_next, *, tm):
    n, d = x.shape
    row = lambda w: pl.BlockSpec((tm, w), lambda i: (i, 0))
    const = lambda s: pl.BlockSpec(s, lambda i: (0, 0), pipeline_mode=pl.Buffered(1))
    return pl.pallas_call(
        _merge_kernel,
        grid=(n // tm,),
        in_specs=[row(d), row(oa.shape[1]), row(ob.shape[1]),
                  pl.BlockSpec((tm, d), lambda i: (i, 0)), pl.BlockSpec((tm, d), lambda i: (i, 1)),
                  const(wa.shape), const(wb.shape), const(wo.shape), const((1, d))],
        out_specs=[row(d), row(d)],
        out_shape=[jax.ShapeDtypeStruct((n, d), F32), jax.ShapeDtypeStruct((n, d), BF16)],
        compiler_params=_params(("parallel",)),
    )(x, oa, ob, gates, gates, wa, wb, wo, g_next.reshape(1, d))


def _rope_tables(pos, head_dim):
    rot = head_dim // ROT_FRACTION
    half = rot // 2
    inv = 1.0 / (ROPE_THETA ** (jnp.arange(half, dtype=F32) * 2.0 / rot))
    ang = pos.astype(F32)[:, None] * inv[None, :]
    cos, sin = jnp.cos(ang), jnp.sin(ang)
    lane = np.arange(LANES) % head_dim
    first, second = lane < half, (lane >= half) & (lane < rot)
    freq = np.where(second, lane - half, np.where(first, lane, 0))
    cos_t = jnp.where(first | second, cos[:, freq], 1.0)
    sin_lo = jnp.where(first, -sin[:, freq], 0.0)
    sin_hi = jnp.where(second, sin[:, freq], 0.0)
    return cos_t, sin_lo, sin_hi


def _suffix_sum_rhs(n):
    strict_lower = (np.arange(n)[:, None] > np.arange(n)[None, :]).astype(np.float32)
    return np.concatenate([strict_lower, np.ones((n, n), np.float32)], axis=1)


def _row_tile(n, pref):
    return pref if n % pref == 0 else n


def _pages_per_step(n_pages, pref):
    while n_pages % pref:
        pref //= 2
    return pref


def kernel(x_prompt, x_sample, cache_sb_k, cache_sb_v, cache_dsa_k, cache_dsa_v, cache_idx_k, page_table, ffn1_g, ffn1_w1, ffn1_w3, ffn1_w2, mix_g, w_in, w_br_sb, w_br_dsa, w_out, ffn2_g, ffn2_w1, ffn2_w3, ffn2_w2, final_g):
    bsz, seq, d = x_prompt.shape
    nb, n_new, _ = x_sample.shape
    assert n_new == 1, "the sample path is written for one new token per sequence"
    depth = ffn1_g.shape[0]
    n_sb = cache_sb_k.shape[3]
    n_kv = cache_dsa_k.shape[3]
    w_sb = n_sb * HEAD_DIM
    w_dq = w_br_dsa.shape[1]
    n_dsa = w_dq // HEAD_DIM
    wkv = n_kv * HEAD_DIM
    n_idx = (w_in.shape[2] - 3 * w_sb - w_dq - 2 * wkv - IDX_DIM - 2 * d) // (IDX_DIM + 1)
    w_iq = n_idx * IDX_DIM
    assert w_sb == w_dq == w_iq and 2 * wkv + 2 * LANES <= w_sb
    n_pages = page_table.shape[1]
    past = n_pages * PAGE_SIZE
    topk_p = min(TOPK_MAX, seq // 4)
    topk_s = min(TOPK_MAX, (past + n_new) // 4)

    pos_p = jnp.arange(seq)
    pos_s = jnp.full((nb,), past, jnp.int32)
    tabs_p = _rope_tables(pos_p, HEAD_DIM) + _rope_tables(pos_p, IDX_DIM)
    tabs_s = _rope_tables(pos_s, HEAD_DIM) + _rope_tables(pos_s, IDX_DIM)
    tq_sb = _row_tile(seq, 128)
    sum_rhs_q = jnp.asarray(_suffix_sum_rhs(tq_sb), BF16)
    sum_rhs_page = jnp.asarray(_suffix_sum_rhs(PAGE_SIZE), BF16)

    xp = x_prompt.reshape(bsz * seq, d)
    xs = x_sample.reshape(nb, d)
    tm_p = _row_tile(bsz * seq, 512)
    tm_in = _row_tile(seq, 512)
    tm_merge = _row_tile(bsz * seq, 256)
    tf = _row_tile(ffn1_w1.shape[2], 512)
    st_p = [[] for _ in range(5)]
    st_s = [[] for _ in range(5)]

    hp = _rms_call(xp, ffn1_g[0], tm_p)
    hs = _rms_call(xs, ffn1_g[0], nb)
    for layer in range(depth):
        bf = lambda w: w[layer].astype(BF16)
        w1a, w3a, w2a = bf(ffn1_w1), bf(ffn1_w3), bf(ffn1_w2)
        w1b, w3b, w2b = bf(ffn2_w1), bf(ffn2_w3), bf(ffn2_w2)
        wl = w_in[layer]
        off = np.cumsum([0, w_sb, w_sb, w_sb, w_dq, wkv, wkv, w_iq, IDX_DIM, n_idx, d, d])
        col = lambda k: wl[:, off[k]:off[k + 1]]
        pad = lambda w: jnp.zeros((d, w), F32)
        misc = jnp.concatenate([col(4), col(5), col(7), pad(LANES - IDX_DIM), col(8), pad(LANES - n_idx),
                                pad(w_sb - 2 * wkv - 2 * LANES)], axis=1)
        w_tiles = jnp.stack([col(0), col(1), col(2), col(3), col(6), misc]).astype(BF16)
        w_gates = wl[:, off[9]:off[11]].astype(BF16)
        wa, wb, wo = bf(w_br_sb), bf(w_br_dsa), bf(w_out)

        xp, hp = _ffn_call(xp, hp, w1a, w3a, w2a, mix_g[layer], final=False, tm=tm_p, tf=tf)
        xs, hs = _ffn_call(xs, hs, w1a, w3a, w2a, mix_g[layer], final=False, tm=nb, tf=tf)

        (qa, ka, kab, va, vab, qb_t, qi_t, kb, kbb, vb, vb_t, ki, kib, wi_t) = _inproj_call(
            hp, w_tiles, tabs_p, tm=tm_in, n_tab_blocks=seq // tm_in, wkv=wkv, n_idx_heads=n_idx,
            query_on_lanes=True)
        gates = _gates_call(hp, w_gates, tm=tm_p, tn=_row_tile(2 * d, 1024))
        oa = _sb_prompt_call(qa, kab, vab, sum_rhs_q, bsz=bsz, seq=seq, n_heads=n_sb, tq=tq_sb)
        ob = _dsa_prompt_call(qi_t, wi_t, kib, qb_t, kbb, vb_t, bsz=bsz, seq=seq, n_idx_heads=n_idx,
                              n_heads=n_dsa, n_kv=n_kv, topk=topk_p, tq=_row_tile(seq, 128))
        xp, hp = _merge_call(xp, oa, ob, gates, wa, wb, wo, ffn2_g[layer], tm=tm_merge)
        for store, val, shape in zip(st_p, (ka, va, kb, vb, ki),
                                     ((n_sb, HEAD_DIM), (n_sb, HEAD_DIM), (n_kv, HEAD_DIM), (n_kv, HEAD_DIM), (IDX_DIM,))):
            store.append(val.reshape((bsz, seq) + shape))

        (qa, ka, kab, va, vab, qb, qi, kb, kbb, vb, vbb, ki, kib, wi) = _inproj_call(
            hs, w_tiles, tabs_s, tm=nb, n_tab_blocks=1, wkv=wkv, n_idx_heads=n_idx, query_on_lanes=False)
        gates = _gates_call(hs, w_gates, tm=nb, tn=_row_tile(2 * d, 1024))
        oa = _sb_sample_call(page_table, qa, cache_sb_k, cache_sb_v, sum_rhs_page, layer=layer)
        keys, meta = _idx_sample_call(page_table, qi, wi, kib, cache_idx_k, layer=layer,
                                      pages_per_step=_pages_per_step(n_pages, 32), topk=topk_s)
        ob = _dsa_sample_call(page_table, keys, meta, qb, kbb, vbb, cache_dsa_k, cache_dsa_v, layer=layer,
                              pages_per_step=_pages_per_step(n_pages, 16))
        xs, hs = _merge_call(xs, oa, ob, gates, wa, wb, wo, ffn2_g[layer], tm=nb)
        for store, val, shape in zip(st_s, (ka, va, kb, vb, ki),
                                     ((n_sb, HEAD_DIM), (n_sb, HEAD_DIM), (n_kv, HEAD_DIM), (n_kv, HEAD_DIM), (IDX_DIM,))):
            store.append(val.reshape((nb, n_new) + shape))

        last = layer == depth - 1
        g_next = final_g if last else ffn1_g[layer + 1]
        if last:
            xp = _ffn_call(xp, hp, w1b, w3b, w2b, g_next, final=True, tm=tm_p, tf=tf)
            xs = _ffn_call(xs, hs, w1b, w3b, w2b, g_next, final=True, tm=nb, tf=tf)
        else:
            xp, hp = _ffn_call(xp, hp, w1b, w3b, w2b, g_next, final=False, tm=tm_p, tf=tf)
            xs, hs = _ffn_call(xs, hs, w1b, w3b, w2b, g_next, final=False, tm=nb, tf=tf)

    return (xp.reshape(bsz, seq, d), xs.reshape(nb, n_new, d),
            jnp.stack(st_p[0]), jnp.stack(st_p[1]), jnp.stack(st_p[2]), jnp.stack(st_p[3]), jnp.stack(st_p[4]),
            jnp.stack(st_s[0]), jnp.stack(st_s[1]), jnp.stack(st_s[2]), jnp.stack(st_s[3]), jnp.stack(st_s[4]))
```

```python
import functools

import jax
import jax.numpy as jnp
import numpy as np
from jax import lax
from jax.experimental import pallas as pl
from jax.experimental.pallas import tpu as pltpu

HEAD_DIM = 128
IDX_DIM = 64
TOPK_MAX = 256
ROPE_THETA = 500000.0
ROT_FRACTION = 4
EPS = 1e-6
PAGE_SIZE = 128
SCALE = HEAD_DIM ** -0.5
LOG2_E = 1.4426950408889634

LANES = 128
SUBLANES = 8
VMEM_LIMIT = 56 * 1024 * 1024

INT_MIN = -(2 ** 31)
NEG_KEY = -2139095041
SOFTMAX_M_INIT = -1e30
MASK_BIAS = -2e30
SB_LOG_UNDERFLOW = -110.0
DSA_KEY_BLOCK = 512

ROWS_FFN = 512
FFN_HIDDEN_TILE = 512
ROWS_INPROJ = 512
ROWS_MERGE = 256
GATE_COLS = 1024
SB_BLOCK = 128
DSA_QUERY_BLOCK = 128
IDX_POOL_PAGES = 64
DSA_PAGES = 16

F32 = jnp.float32
BF16 = jnp.bfloat16
I32 = jnp.int32

_NT = (((1,), (1,)), ((), ()))


def _params(sem):
    return pltpu.CompilerParams(dimension_semantics=sem, vmem_limit_bytes=VMEM_LIMIT)


def _rms(x, g):
    return x * lax.rsqrt(jnp.mean(x * x, axis=-1, keepdims=True) + EPS) * g


def _log_sigmoid(z):
    return jnp.minimum(z, 0.0) - jnp.log(1.0 + jnp.exp(-jnp.abs(z)))


def _order_key(x):
    bits = pltpu.bitcast(x, I32)
    bits = jnp.where(bits == INT_MIN, 0, bits)
    return jnp.where(bits < 0, bits ^ 0x7FFFFFFF, bits)


def _split_bf16(x):
    hi = x.astype(BF16)
    lo = (x - hi.astype(F32)).astype(BF16)
    return hi, lo


def _rms_kernel(x_ref, g_ref, h_ref):
    h_ref[...] = _rms(x_ref[...], g_ref[...]).astype(BF16)


def _rms_call(x, g, tm):
    n, d = x.shape
    return pl.pallas_call(
        _rms_kernel,
        grid=(n // tm,),
        in_specs=[pl.BlockSpec((tm, d), lambda i: (i, 0)), pl.BlockSpec((1, d), lambda i: (0, 0))],
        out_specs=pl.BlockSpec((tm, d), lambda i: (i, 0)),
        out_shape=jax.ShapeDtypeStruct((n, d), BF16),
        compiler_params=_params(("parallel",)),
    )(x, g.reshape(1, d))


def _ffn_kernel(x_ref, h_ref, w1_ref, w3_ref, w2_ref, g_ref, *rest, final):
    if final:
        y_ref, acc_ref = rest
    else:
        y_ref, hn_ref, acc_ref = rest
    f = pl.program_id(1)

    @pl.when(f == 0)
    def _():
        acc_ref[...] = jnp.zeros_like(acc_ref)

    h = h_ref[...]
    a = jnp.dot(h, w1_ref[...], preferred_element_type=F32)
    b = jnp.dot(h, w3_ref[...], preferred_element_type=F32)
    u = (a * jax.nn.sigmoid(a)) * b
    acc_ref[...] += jnp.dot(u.astype(BF16), w2_ref[...], preferred_element_type=F32)

    @pl.when(f == pl.num_programs(1) - 1)
    def _():
        y = x_ref[...] + 0.5 * acc_ref[...]
        if final:
            y_ref[...] = _rms(y, g_ref[...])
        else:
            y_ref[...] = y
            hn_ref[...] = _rms(y, g_ref[...]).astype(BF16)


def _ffn_call(x, h, w1, w3, w2, g_next, *, final, tm, tf):
    n, d = x.shape
    dff = w1.shape[1]
    out_shape = [jax.ShapeDtypeStruct((n, d), F32)]
    out_specs = [pl.BlockSpec((tm, d), lambda i, f: (i, 0))]
    if not final:
        out_shape.append(jax.ShapeDtypeStruct((n, d), BF16))
        out_specs.append(pl.BlockSpec((tm, d), lambda i, f: (i, 0)))
    res = pl.pallas_call(
        functools.partial(_ffn_kernel, final=final),
        grid=(n // tm, dff // tf),
        in_specs=[
            pl.BlockSpec((tm, d), lambda i, f: (i, 0)),
            pl.BlockSpec((tm, d), lambda i, f: (i, 0)),
            pl.BlockSpec((d, tf), lambda i, f: (0, f)),
            pl.BlockSpec((d, tf), lambda i, f: (0, f)),
            pl.BlockSpec((tf, d), lambda i, f: (f, 0)),
            pl.BlockSpec((1, d), lambda i, f: (0, 0)),
        ],
        out_specs=out_specs,
        out_shape=out_shape,
        scratch_shapes=[pltpu.VMEM((tm, d), F32)],
        compiler_params=_params(("parallel", "arbitrary")),
    )(x, h, w1, w3, w2, g_next.reshape(1, d))
    return res[0] if final else res


def _rope(x, cos, sin_lo, sin_hi, half):
    return (x * cos + pltpu.roll(x, LANES - half, axis=1) * sin_lo
            + pltpu.roll(x, half, axis=1) * sin_hi)


def _inproj_kernel(h_ref, w_ref, cd_ref, sdl_ref, sdh_ref, ci_ref, sil_ref, sih_ref,
                   qa_ref, ka_ref, kab_ref, va_ref, vab_ref, qb_ref, qi_ref,
                   kb_ref, kbb_ref, vb_ref, vbb_ref, ki_ref, kib_ref, wi_ref, *, wkv, query_on_lanes):
    j = pl.program_id(1)
    acc = jnp.dot(h_ref[...], w_ref[0], preferred_element_type=F32)
    lay = (lambda x: x.T) if query_on_lanes else (lambda x: x)
    rot_d = HEAD_DIM // ROT_FRACTION // 2
    rot_i = IDX_DIM // ROT_FRACTION // 2

    def rope_cols(x, idx):
        if idx:
            c, sl, sh, half = ci_ref[...], sil_ref[...], sih_ref[...], rot_i
        else:
            c, sl, sh, half = cd_ref[...], sdl_ref[...], sdh_ref[...], rot_d
        return jnp.concatenate(
            [_rope(x[:, s:s + LANES], c, sl, sh, half) for s in range(0, x.shape[1], LANES)], axis=1)

    @pl.when(j == 0)
    def _():
        qa_ref[...] = acc.astype(BF16)

    @pl.when(j == 1)
    def _():
        ka_ref[...] = acc
        kab_ref[...] = acc.astype(BF16)

    @pl.when(j == 2)
    def _():
        va_ref[...] = acc
        vab_ref[...] = acc.astype(BF16)

    @pl.when(j == 3)
    def _():
        qb_ref[...] = lay(rope_cols(acc, False)).astype(BF16)

    @pl.when(j == 4)
    def _():
        qi_ref[...] = lay(rope_cols(acc, True)).astype(BF16)

    @pl.when(j == 5)
    def _():
        kb = rope_cols(acc[:, :wkv], False)
        kb_ref[...] = kb
        kbb_ref[...] = kb.astype(BF16)
        vb = acc[:, wkv:2 * wkv]
        vb_ref[...] = vb
        ki = rope_cols(acc[:, 2 * wkv:2 * wkv + LANES], True)[:, :IDX_DIM]
        ki_ref[...] = ki
        kib_ref[...] = ki.astype(BF16)
        wi_lanes = acc[:, 2 * wkv + LANES:2 * wkv + 2 * LANES]
        if query_on_lanes:
            vb_t = vb.T.astype(BF16)
            for c in range(vbb_ref.shape[0]):
                vbb_ref[c] = vb_t[:, c * vbb_ref.shape[2]:(c + 1) * vbb_ref.shape[2]]
            wi_ref[...] = wi_lanes.T[:wi_ref.shape[0], :]
        else:
            vbb_ref[...] = vb.astype(BF16)
            wi_ref[...] = wi_lanes[:, :wi_ref.shape[1]]


def _inproj_call(h, w_tiles, tabs, *, tm, n_tab_blocks, wkv, n_idx_heads, query_on_lanes):
    n, d = h.shape
    nt, _, tw = w_tiles.shape
    row = lambda i, j: (i, 0)
    tab = lambda i, j: (i % n_tab_blocks, 0)
    rows_major = lambda w, dt: (jax.ShapeDtypeStruct((n, w), dt), pl.BlockSpec((tm, w), row))
    if query_on_lanes:
        lanes_major = lambda w, dt: (jax.ShapeDtypeStruct((w, n), dt), pl.BlockSpec((w, tm), lambda i, j: (0, i)))
        v_blocks = (jax.ShapeDtypeStruct((n // DSA_KEY_BLOCK, wkv, DSA_KEY_BLOCK), BF16),
                    pl.BlockSpec((tm // DSA_KEY_BLOCK, wkv, DSA_KEY_BLOCK), lambda i, j: (i, 0, 0)))
    else:
        lanes_major = rows_major
        v_blocks = rows_major(wkv, BF16)
    outs = [rows_major(tw, BF16), rows_major(tw, F32), rows_major(tw, BF16), rows_major(tw, F32),
            rows_major(tw, BF16), lanes_major(tw, BF16), lanes_major(tw, BF16),
            rows_major(wkv, F32), rows_major(wkv, BF16), rows_major(wkv, F32), v_blocks,
            rows_major(IDX_DIM, F32), rows_major(IDX_DIM, BF16), lanes_major(n_idx_heads, F32)]
    return pl.pallas_call(
        functools.partial(_inproj_kernel, wkv=wkv, query_on_lanes=query_on_lanes),
        grid=(n // tm, nt),
        in_specs=[pl.BlockSpec((tm, d), row), pl.BlockSpec((1, d, tw), lambda i, j: (j, 0, 0))]
        + [pl.BlockSpec((tm, LANES), tab)] * 6,
        out_specs=[spec for _, spec in outs],
        out_shape=[shape for shape, _ in outs],
        compiler_params=_params(("parallel", "arbitrary")),
    )(h, w_tiles, *tabs)


def _gates_kernel(h_ref, w_ref, o_ref):
    o_ref[...] = jax.nn.sigmoid(jnp.dot(h_ref[...], w_ref[...], preferred_element_type=F32))


def _gates_call(h, wg, *, tm, tn):
    n, d = h.shape
    ncol = wg.shape[1]
    return pl.pallas_call(
        _gates_kernel,
        grid=(ncol // tn, n // tm),
        in_specs=[pl.BlockSpec((tm, d), lambda j, i: (i, 0)), pl.BlockSpec((d, tn), lambda j, i: (0, j))],
        out_specs=pl.BlockSpec((tm, tn), lambda j, i: (i, j)),
        out_shape=jax.ShapeDtypeStruct((n, ncol), F32),
        compiler_params=_params(("parallel", "arbitrary")),
    )(h, wg)


def _sb_weights(z, sum_rhs, carry, mask):
    n = z.shape[1]
    ls = _log_sigmoid(z)
    lr = ls - z
    if mask is not None:
        lr = jnp.where(mask, lr, 0.0)
    hi, lo = _split_bf16(lr)
    sums = (jnp.dot(hi, sum_rhs, preferred_element_type=F32)
            + jnp.dot(lo, sum_rhs, preferred_element_type=F32))
    a = jnp.exp(ls + sums[:, :n] + carry)
    if mask is not None:
        a = jnp.where(mask, a, 0.0)
    return a, carry + sums[:, n:]


def _sb_prompt_kernel(q_ref, k_ref, v_ref, rhs_ref, o_ref, carry_ref, acc_ref, *, tq, n_heads):
    qi = pl.program_id(1)
    rhs = rhs_ref[...]
    rows = lax.broadcasted_iota(I32, (n_heads * tq, tq), 0) % tq
    cols = lax.broadcasted_iota(I32, (n_heads * tq, tq), 1)
    head_lanes = [slice(h * HEAD_DIM, (h + 1) * HEAD_DIM) for h in range(n_heads)]

    def block(kb, mask, first):
        ks = pl.multiple_of(kb * tq, tq)
        z = jnp.concatenate(
            [lax.dot_general(q_ref[:, hl], k_ref[pl.ds(ks, tq), hl], _NT, preferred_element_type=F32)
             for hl in head_lanes], axis=0) * SCALE
        a, carry = _sb_weights(z, rhs, 0.0 if first else carry_ref[...], mask)
        carry_ref[...] = carry
        ab = a.astype(BF16)
        pv = jnp.concatenate(
            [jnp.dot(ab[h * tq:(h + 1) * tq], v_ref[pl.ds(ks, tq), hl], preferred_element_type=F32)
             for h, hl in enumerate(head_lanes)], axis=0)
        acc_ref[...] = pv if first else acc_ref[...] + pv
        return jnp.max(carry)

    def more(state):
        return (state[0] >= 0) & (state[1] > SB_LOG_UNDERFLOW)

    def step(state):
        return state[0] - 1, block(state[0], None, False)

    lax.while_loop(more, step, (qi - 1, block(qi, cols < rows, True)))
    for h, hl in enumerate(head_lanes):
        o_ref[:, hl] = acc_ref[h * tq:(h + 1) * tq, :].astype(BF16)


def _sb_prompt_call(qa, ka, va, sum_rhs, *, bsz, seq, n_heads, tq):
    width = n_heads * HEAD_DIM
    q3 = qa.reshape(bsz, seq, width)
    k3 = ka.reshape(bsz, seq, width)
    v3 = va.reshape(bsz, seq, width)
    qspec = pl.BlockSpec((None, tq, width), lambda b, i: (b, i, 0))
    kspec = pl.BlockSpec((None, seq, width), lambda b, i: (b, 0, 0), pipeline_mode=pl.Buffered(1))
    out = pl.pallas_call(
        functools.partial(_sb_prompt_kernel, tq=tq, n_heads=n_heads),
        grid=(bsz, seq // tq),
        in_specs=[qspec, kspec, kspec, pl.BlockSpec((tq, 2 * tq), lambda b, i: (0, 0))],
        out_specs=qspec,
        out_shape=jax.ShapeDtypeStruct((bsz, seq, width), BF16),
        scratch_shapes=[pltpu.VMEM((n_heads * tq, tq), F32), pltpu.VMEM((n_heads * tq, HEAD_DIM), F32)],
        compiler_params=_params(("parallel", "arbitrary")),
    )(q3, k3, v3, sum_rhs)
    return out.reshape(bsz * seq, width)


def _kth_largest_key(count_ge, shape, topk):
    def body(it, pat):
        cand = pat ^ jnp.left_shift(jnp.int32(1), 31 - it)
        return jnp.where(count_ge(cand) >= topk, cand, pat)

    return lax.fori_loop(0, 32, body, jnp.full(shape, INT_MIN, I32))


def _tie_cut(count_eq_before, need, shape, nbits):
    def body(it, c):
        cand = c | jnp.left_shift(jnp.int32(1), nbits - 1 - it)
        return jnp.where(count_eq_before(cand) < need, cand, c)

    return lax.fori_loop(0, nbits, body, jnp.zeros(shape, I32)) + 1


def _dsa_prompt_kernel(qi_ref, wi_ref, ki_ref, qb_ref, kb_ref, vb_ref, o_ref,
                       key_ref, m_ref, l_ref, acc_ref, *, tq, tk, topk, n_idx_heads, n_kv, rep, pos_bits):
    i = pl.program_id(1)
    q0 = i * tq
    nkb = (q0 + tq + tk - 1) // tk
    kpos = lax.broadcasted_iota(I32, (tk, tq), 0)
    qpos = q0 + lax.broadcasted_iota(I32, (tk, tq), 1)

    pair_rhs = [jnp.concatenate([qi_ref[h * IDX_DIM:(h + 1) * IDX_DIM, :],
                                 qi_ref[(h + 1) * IDX_DIM:(h + 2) * IDX_DIM, :]], axis=1)
                for h in range(0, n_idx_heads, 2)]
    w_row = [wi_ref[h:h + 1, :] for h in range(n_idx_heads)]

    def score_block(kb, _):
        ks = pl.multiple_of(kb * tk, tk)
        kblk = ki_ref[pl.ds(ks, tk), :]
        sc = jnp.zeros((tk, tq), F32)
        for p, rhs in enumerate(pair_rhs):
            logit = jnp.dot(kblk, rhs, preferred_element_type=F32)
            sc = (sc + jnp.maximum(logit[:, :tq], 0.0) * w_row[2 * p]
                  + jnp.maximum(logit[:, tq:], 0.0) * w_row[2 * p + 1])
        sc = jnp.where(ks + kpos <= qpos, sc, -jnp.inf)
        key_ref[pl.ds(ks, tk), :] = _order_key(sc)
        return 0

    lax.fori_loop(0, nkb, score_block, 0)

    def count(pred):
        def body(kb, cnt):
            ks = pl.multiple_of(kb * tk, tk)
            hit = pred(key_ref[pl.ds(ks, tk), :], ks + kpos).astype(I32)
            return cnt + jnp.sum(hit.reshape(tk // SUBLANES, SUBLANES, tq), axis=0)

        cnt = lax.fori_loop(0, nkb, body, jnp.zeros((SUBLANES, tq), I32))
        return jnp.sum(cnt, axis=0, keepdims=True)

    count_ge = lambda v: count(lambda key, pos: key >= v)
    kth = _kth_largest_key(count_ge, (1, tq), topk)
    thr = jnp.maximum(kth, NEG_KEY + 1)

    n_ge = count_ge(thr)

    @pl.when(jnp.max(n_ge) > topk)
    def _():
        need = topk - count(lambda key, pos: key > thr)
        cut = _tie_cut(lambda c: count(lambda key, pos: (key == thr) & (pos < c)), need, (1, tq), pos_bits)

        def drop(kb, _):
            ks = pl.multiple_of(kb * tk, tk)
            key = key_ref[pl.ds(ks, tk), :]
            key_ref[pl.ds(ks, tk), :] = jnp.where((key == thr) & (ks + kpos >= cut), NEG_KEY, key)
            return 0

        lax.fori_loop(0, nkb, drop, 0)

    m_ref[...] = jnp.full(m_ref.shape, SOFTMAX_M_INIT, F32)
    l_ref[...] = jnp.zeros_like(l_ref)
    acc_ref[...] = jnp.zeros_like(acc_ref)
    q_cols = [jnp.concatenate([qb_ref[(g * rep + r) * HEAD_DIM:(g * rep + r + 1) * HEAD_DIM, :]
                               for r in range(rep)], axis=1) for g in range(n_kv)]

    def attend(kb, _):
        ks = pl.multiple_of(kb * tk, tk)
        bias = jnp.where(key_ref[pl.ds(ks, tk), :] >= thr, 0.0, MASK_BIAS)
        bias = jnp.concatenate([bias] * rep, axis=1)
        s = [jnp.dot(kb_ref[pl.ds(ks, tk), g * HEAD_DIM:(g + 1) * HEAD_DIM], q_cols[g],
                     preferred_element_type=F32) * (SCALE * LOG2_E) + bias for g in range(n_kv)]
        m_old = [m_ref[g] for g in range(n_kv)]
        m_new = [jnp.maximum(m_old[g], jnp.max(s[g], axis=0, keepdims=True)) for g in range(n_kv)]
        p = [jnp.exp2(s[g] - m_new[g]) for g in range(n_kv)]
        for g in range(n_kv):
            alpha = jnp.exp2(m_old[g] - m_new[g])
            l_ref[g] = alpha * l_ref[g] + jnp.sum(p[g], axis=0, keepdims=True)
            v_t = vb_ref[kb, g * HEAD_DIM:(g + 1) * HEAD_DIM, :]
            acc_ref[g] = alpha * acc_ref[g] + jnp.dot(v_t, p[g].astype(BF16), preferred_element_type=F32)
            m_ref[g] = m_new[g]
        return 0

    lax.fori_loop(0, nkb, attend, 0)
    for g in range(n_kv):
        out_t = acc_ref[g] / l_ref[g]
        for r in range(rep):
            h = g * rep + r
            o_ref[:, h * HEAD_DIM:(h + 1) * HEAD_DIM] = out_t[:, r * tq:(r + 1) * tq].T.astype(BF16)


def _dsa_prompt_call(qi_t, wi_t, ki, qb_t, kb, vb_t, *, bsz, seq, n_idx_heads, n_heads, n_kv, topk, tq):
    rep = n_heads // n_kv
    nq = seq // tq
    tk = vb_t.shape[2]
    wkv = n_kv * HEAD_DIM
    wq, wb = n_idx_heads * IDX_DIM, n_heads * HEAD_DIM
    qspec = lambda w: pl.BlockSpec((w, tq), lambda b, i: (0, b * nq + i))
    kspec = lambda w: pl.BlockSpec((None, seq, w), lambda b, i: (b, 0, 0))
    out = pl.pallas_call(
        functools.partial(_dsa_prompt_kernel, tq=tq, tk=tk, topk=topk, n_idx_heads=n_idx_heads,
                          n_kv=n_kv, rep=rep, pos_bits=max(1, int(seq - 1).bit_length())),
        grid=(bsz, nq),
        in_specs=[qspec(wq), qspec(n_idx_heads), kspec(IDX_DIM), qspec(wb), kspec(wkv),
                  pl.BlockSpec((seq // tk, wkv, tk), lambda b, i: (b, 0, 0))],
        out_specs=pl.BlockSpec((None, tq, wb), lambda b, i: (b, i, 0)),
        out_shape=jax.ShapeDtypeStruct((bsz, seq, wb), BF16),
        scratch_shapes=[pltpu.VMEM((seq, tq), I32),
                        pltpu.VMEM((n_kv, 1, rep * tq), F32),
                        pltpu.VMEM((n_kv, 1, rep * tq), F32),
                        pltpu.VMEM((n_kv, HEAD_DIM, rep * tq), F32)],
        compiler_params=_params(("parallel", "arbitrary")),
    )(qi_t, wi_t, ki.reshape(bsz, seq, IDX_DIM), qb_t, kb.reshape(bsz, seq, wkv), vb_t)
    return out.reshape(bsz * seq, wb)


def _sb_sample_kernel(pt_ref, q_ref, rhs_ref, ck_ref, cv_ref, o_ref, kbuf, vbuf, sem, *, layer, n_heads, n_pages):
    b = pl.program_id(0)
    q = q_ref[...]
    rhs = rhs_ref[...]
    head = lax.broadcasted_iota(I32, (n_heads, PAGE_SIZE), 0)

    def page_copies(n, slot):
        page = pt_ref[b, n_pages - 1 - n]
        return (pltpu.make_async_copy(ck_ref.at[layer, page], kbuf.at[slot], sem.at[0, slot]),
                pltpu.make_async_copy(cv_ref.at[layer, page], vbuf.at[slot], sem.at[1, slot]))

    def start(n, slot):
        for copy in page_copies(n, slot):
            copy.start()

    def wait(n, slot):
        for copy in page_copies(n, slot):
            copy.wait()

    def more(state):
        return (state[0] < n_pages) & (state[1] > SB_LOG_UNDERFLOW)

    def step(state):
        n, _, carry, acc = state
        slot = n % 2
        wait(n, slot)

        @pl.when(n + 1 < n_pages)
        def _():
            start(n + 1, 1 - slot)

        z = jnp.zeros((n_heads, PAGE_SIZE), F32)
        for h in range(n_heads):
            kh = kbuf[slot, pl.ds(h, PAGE_SIZE, stride=n_heads), :].astype(BF16)
            z = jnp.where(head == h, lax.dot_general(q, kh, _NT, preferred_element_type=F32), z)
        a, carry = _sb_weights(z * SCALE, rhs, carry, None)
        ab = a.astype(BF16)
        for h in range(n_heads):
            vh = vbuf[slot, pl.ds(h, PAGE_SIZE, stride=n_heads), :].astype(BF16)
            acc = acc + jnp.where(head == h, jnp.dot(ab, vh, preferred_element_type=F32), 0.0)
        return n + 1, jnp.max(carry), carry, acc

    start(0, 0)
    n_done, _, _, acc = lax.while_loop(
        more, step, (jnp.int32(0), jnp.float32(0.0), jnp.zeros((n_heads, PAGE_SIZE), F32),
                     jnp.zeros((n_heads, HEAD_DIM), F32)))

    @pl.when(n_done < n_pages)
    def _():
        wait(n_done, n_done % 2)

    o_ref[...] = acc.astype(BF16)


def _page_specs(block, n_steps_pages, pages_per_step, layer, reverse):
    specs = []
    for p in range(pages_per_step):
        if reverse:
            pick = lambda b, j, pt, p=p: pt[b, n_steps_pages - 1 - (j * pages_per_step + p)]
        else:
            pick = lambda b, j, pt, p=p: pt[b, j * pages_per_step + p]
        zeros = (0,) * (len(block) - 2)
        specs.append(pl.BlockSpec(block, lambda b, j, pt, pick=pick: (layer, pick(b, j, pt)) + zeros))
    return specs


def _flat_pages(cache):
    depth, pool, page, heads, dim = cache.shape
    return cache.reshape(depth, pool, page * heads, dim)


def _sb_sample_call(page_table, qa, cache_k, cache_v, sum_rhs, *, layer):
    nb, n_pages = page_table.shape
    n_heads = cache_k.shape[3]
    q3 = qa.reshape(nb, n_heads, HEAD_DIM)
    row = pl.BlockSpec((None, n_heads, HEAD_DIM), lambda b, pt: (b, 0, 0))
    page_rows = PAGE_SIZE * n_heads
    out = pl.pallas_call(
        functools.partial(_sb_sample_kernel, layer=layer, n_heads=n_heads, n_pages=n_pages),
        grid_spec=pltpu.PrefetchScalarGridSpec(
            num_scalar_prefetch=1,
            grid=(nb,),
            in_specs=[row, pl.BlockSpec((PAGE_SIZE, 2 * PAGE_SIZE), lambda b, pt: (0, 0)),
                      pl.BlockSpec(memory_space=pl.ANY), pl.BlockSpec(memory_space=pl.ANY)],
            out_specs=row,
            scratch_shapes=[pltpu.VMEM((2, page_rows, HEAD_DIM), F32), pltpu.VMEM((2, page_rows, HEAD_DIM), F32),
                            pltpu.SemaphoreType.DMA((2, 2))],
        ),
        out_shape=jax.ShapeDtypeStruct((nb, n_heads, HEAD_DIM), BF16),
        compiler_params=_params(("arbitrary",)),
    )(page_table, q3, sum_rhs, _flat_pages(cache_k), _flat_pages(cache_v))
    return out.reshape(nb, n_heads * HEAD_DIM)


def _idx_pool_kernel(owner_ref, qi_ref, wi_ref, pool_ref, key_ref, *, pages_per_block, n_pages):
    j = pl.program_id(0)

    @pl.when(j == 0)
    def _():
        key_ref[...] = jnp.full(key_ref.shape, NEG_KEY, I32)

    def score_page(p, _):
        slot = owner_ref[j * pages_per_block + p]

        @pl.when(slot >= 0)
        def _():
            b = slot // n_pages
            logit = jnp.dot(qi_ref[b], pool_ref[p].astype(BF16), preferred_element_type=F32)
            sc = jnp.sum(jnp.maximum(logit, 0.0) * wi_ref[b], axis=0, keepdims=True)
            key_ref[b, pl.ds(slot - b * n_pages, 1), :] = _order_key(sc)

        return 0

    lax.fori_loop(0, pages_per_block, score_page, 0, unroll=8)


def _idx_select_kernel(keyin_ref, qi_ref, wi_ref, kin_ref, key_ref, meta_ref, *, topk, pos_bits):
    qi = qi_ref[...]
    wi = wi_ref[...]
    keys = keyin_ref[...]
    n_pages = keys.shape[0]
    past = n_pages * PAGE_SIZE
    pos = (lax.broadcasted_iota(I32, (n_pages, PAGE_SIZE), 0) * PAGE_SIZE
           + lax.broadcasted_iota(I32, (n_pages, PAGE_SIZE), 1))
    own = jnp.sum(qi.astype(F32) * kin_ref[...].astype(F32), axis=1, keepdims=True)
    own_key = _order_key(jnp.sum(jnp.maximum(own, 0.0) * wi, axis=0, keepdims=True))

    def total(x):
        return jnp.sum(jnp.sum(x.astype(I32), axis=1, keepdims=True), axis=0, keepdims=True)

    def count(pred):
        return total(pred(keys, pos)) + pred(own_key, past).astype(I32)

    count_ge = lambda v: count(lambda key, where: key >= v)
    thr = _kth_largest_key(count_ge, (1, 1), topk)
    need = topk - count(lambda key, where: key > thr)
    cut = _tie_cut(lambda c: count(lambda key, where: (key == thr) & (where < c)), need, (1, 1), pos_bits)
    key_ref[...] = jnp.where((keys == thr) & (pos >= cut), NEG_KEY, keys)
    own_sel = (own_key > thr) | ((own_key == thr) & (past < cut))
    rowid = lax.broadcasted_iota(I32, meta_ref.shape, 0)
    meta_ref[...] = jnp.where(rowid == 0, thr, own_sel.astype(I32))


def _idx_sample_call(page_table, qi, wi, ki_new, cache_idx, *, layer, topk):
    nb, n_pages = page_table.shape
    nh = wi.shape[1]
    n_pool = cache_idx.shape[1]
    cache_t = jnp.swapaxes(cache_idx, 2, 3)
    owner = jnp.full((n_pool,), -1, I32).at[page_table.reshape(-1)].set(jnp.arange(nb * n_pages, dtype=I32))
    q3, w3 = qi.reshape(nb, nh, IDX_DIM), wi.reshape(nb, nh, 1)
    ppb = _pages_per_step(n_pool, IDX_POOL_PAGES)
    full = lambda s: pl.BlockSpec(s, lambda j, own: (0,) * len(s))
    keys = pl.pallas_call(
        functools.partial(_idx_pool_kernel, pages_per_block=ppb, n_pages=n_pages),
        grid_spec=pltpu.PrefetchScalarGridSpec(
            num_scalar_prefetch=1,
            grid=(n_pool // ppb,),
            in_specs=[full((nb, nh, IDX_DIM)), full((nb, nh, 1)),
                      pl.BlockSpec((None, ppb, IDX_DIM, PAGE_SIZE), lambda j, own: (layer, j, 0, 0))],
            out_specs=full((nb, n_pages, PAGE_SIZE)),
        ),
        out_shape=jax.ShapeDtypeStruct((nb, n_pages, PAGE_SIZE), I32),
        compiler_params=_params(("arbitrary",)),
    )(owner, q3, w3, cache_t)
    row = lambda s: pl.BlockSpec((None,) + s, lambda b: (b, 0, 0))
    return pl.pallas_call(
        functools.partial(_idx_select_kernel, topk=topk, pos_bits=int(n_pages * PAGE_SIZE).bit_length()),
        grid=(nb,),
        in_specs=[row((n_pages, PAGE_SIZE)), row((nh, IDX_DIM)), row((nh, 1)), row((1, IDX_DIM))],
        out_specs=[row((n_pages, PAGE_SIZE)), row((SUBLANES, LANES))],
        out_shape=[jax.ShapeDtypeStruct((nb, n_pages, PAGE_SIZE), I32),
                   jax.ShapeDtypeStruct((nb, SUBLANES, LANES), I32)],
        compiler_params=_params(("parallel",)),
    )(keys, q3, w3, ki_new.reshape(nb, 1, IDX_DIM))


def _dsa_sample_kernel(pt_ref, key_ref, meta_ref, q_ref, kn_ref, vn_ref, *rest, n_heads, n_kv, pages_per_step):
    del pt_ref
    k_refs = rest[:pages_per_step]
    v_refs = rest[pages_per_step:2 * pages_per_step]
    o_ref, m_ref, l_ref, acc_ref = rest[2 * pages_per_step:]
    j = pl.program_id(1)
    rep = n_heads // n_kv

    @pl.when(j == 0)
    def _():
        m_ref[...] = jnp.full(m_ref.shape, SOFTMAX_M_INIT, F32)
        l_ref[...] = jnp.zeros_like(l_ref)
        acc_ref[...] = jnp.zeros_like(acc_ref)

    q = q_ref[...]
    meta = meta_ref[...]
    thr = meta[0:1, :]
    group = lax.broadcasted_iota(I32, (n_heads, PAGE_SIZE), 0) // rep
    m, l, acc = m_ref[...], l_ref[...], acc_ref[...]

    def update(m, l, acc, s, pv):
        m_new = jnp.maximum(m, jnp.max(s, axis=1, keepdims=True))
        p = jnp.exp(s - m_new)
        alpha = jnp.exp(m - m_new)
        return m_new, alpha * l + jnp.sum(p, axis=1, keepdims=True), alpha * acc + pv(p)

    step_keys = pages_per_step * PAGE_SIZE
    group_all = lax.broadcasted_iota(I32, (n_heads, step_keys), 0) // rep

    def group_rows(refs, g):
        return jnp.concatenate([r[pl.ds(g, PAGE_SIZE, stride=n_kv), :].astype(BF16) for r in refs], axis=0)

    bias = jnp.concatenate(
        [jnp.where(key_ref[pl.ds(j * pages_per_step + p_i, 1), :] >= thr, 0.0, MASK_BIAS)
         for p_i in range(pages_per_step)], axis=1)
    s_all = jnp.zeros((n_heads, step_keys), F32)
    for g in range(n_kv):
        sg = lax.dot_general(q, group_rows(k_refs, g), _NT, preferred_element_type=F32)
        s_all = jnp.where(group_all == g, sg, s_all)

    def pv(p):
        pb = p.astype(BF16)
        out = jnp.zeros((n_heads, HEAD_DIM), F32)
        for g in range(n_kv):
            og = jnp.dot(pb, group_rows(v_refs, g), preferred_element_type=F32)
            out = jnp.where(group == g, og, out)
        return out

    m, l, acc = update(m, l, acc, s_all * SCALE + bias, pv)
    m_ref[...], l_ref[...], acc_ref[...] = m, l, acc

    @pl.when(j == pl.num_programs(1) - 1)
    def _():
        own_bias = jnp.where(meta[1:2, 0:1] > 0, 0.0, MASK_BIAS)
        s_own = jnp.sum(q.astype(F32) * kn_ref[...].astype(F32), axis=1, keepdims=True) * SCALE + own_bias
        vn = vn_ref[...].astype(F32)
        _, l_f, acc_f = update(m, l, acc, s_own, lambda p: p.astype(BF16).astype(F32) * vn)
        o_ref[...] = (acc_f / l_f).astype(BF16)


def _dsa_sample_call(page_table, keys, meta, qb, kb_new, vb_new, cache_k, cache_v, *, layer, pages_per_step):
    nb, n_pages = page_table.shape
    n_kv = cache_k.shape[3]
    n_heads = qb.shape[1] // HEAD_DIM
    rep = n_heads // n_kv
    row = lambda s: pl.BlockSpec((None,) + s, lambda b, j, pt: (b, 0, 0))
    own = lambda t: jnp.repeat(t.reshape(nb, n_kv, HEAD_DIM), rep, axis=1)
    page_block = (None, None, PAGE_SIZE * n_kv, HEAD_DIM)
    kspecs = _page_specs(page_block, n_pages, pages_per_step, layer, False)
    out = pl.pallas_call(
        functools.partial(_dsa_sample_kernel, n_heads=n_heads, n_kv=n_kv, pages_per_step=pages_per_step),
        grid_spec=pltpu.PrefetchScalarGridSpec(
            num_scalar_prefetch=1,
            grid=(nb, n_pages // pages_per_step),
            in_specs=[row((n_pages, PAGE_SIZE)), row((SUBLANES, LANES)), row((n_heads, HEAD_DIM)),
                      row((n_heads, HEAD_DIM)), row((n_heads, HEAD_DIM))] + kspecs + kspecs,
            out_specs=row((n_heads, HEAD_DIM)),
            scratch_shapes=[pltpu.VMEM((n_heads, 1), F32), pltpu.VMEM((n_heads, 1), F32),
                            pltpu.VMEM((n_heads, HEAD_DIM), F32)],
        ),
        out_shape=jax.ShapeDtypeStruct((nb, n_heads, HEAD_DIM), BF16),
        compiler_params=_params(("parallel", "arbitrary")),
    )(page_table, keys, meta, qb.reshape(nb, n_heads, HEAD_DIM), own(kb_new), own(vb_new),
      *([_flat_pages(cache_k)] * pages_per_step), *([_flat_pages(cache_v)] * pages_per_step))
    return out.reshape(nb, n_heads * HEAD_DIM)


def _merge_kernel(x_ref, oa_ref, ob_ref, ga_ref, gb_ref, wa_ref, wb_ref, wo_ref, g_ref, y_ref, hn_ref):
    mixed = (ga_ref[...] * jnp.dot(oa_ref[...], wa_ref[...], preferred_element_type=F32)
             + gb_ref[...] * jnp.dot(ob_ref[...], wb_ref[...], preferred_element_type=F32))
    y = x_ref[...] + jnp.dot(mixed.astype(BF16), wo_ref[...], preferred_element_type=F32)
    y_ref[...] = y
    hn_ref[...] = _rms(y, g_ref[...]).astype(BF16)


def _merge_call(x, oa, ob, gates, wa, wb, wo, g_next, *, tm):
    n, d = x.shape
    row = lambda w: pl.BlockSpec((tm, w), lambda i: (i, 0))
    const = lambda s: pl.BlockSpec(s, lambda i: (0, 0), pipeline_mode=pl.Buffered(1))
    return pl.pallas_call(
        _merge_kernel,
        grid=(n // tm,),
        in_specs=[row(d), row(oa.shape[1]), row(ob.shape[1]),
                  pl.BlockSpec((tm, d), lambda i: (i, 0)), pl.BlockSpec((tm, d), lambda i: (i, 1)),
                  const(wa.shape), const(wb.shape), const(wo.shape), const((1, d))],
        out_specs=[row(d), row(d)],
        out_shape=[jax.ShapeDtypeStruct((n, d), F32), jax.ShapeDtypeStruct((n, d), BF16)],
        compiler_params=_params(("parallel",)),
    )(x, oa, ob, gates, gates, wa, wb, wo, g_next.reshape(1, d))


def _rope_tables(pos, head_dim):
    rot = head_dim // ROT_FRACTION
    half = rot // 2
    inv = 1.0 / (ROPE_THETA ** (jnp.arange(half, dtype=F32) * 2.0 / rot))
    ang = pos.astype(F32)[:, None] * inv[None, :]
    cos, sin = jnp.cos(ang), jnp.sin(ang)
    lane = np.arange(LANES) % head_dim
    first, second = lane < half, (lane >= half) & (lane < rot)
    freq = np.where(second, lane - half, np.where(first, lane, 0))
    cos_t = jnp.where(first | second, cos[:, freq], 1.0)
    sin_lo = jnp.where(first, -sin[:, freq], 0.0)
    sin_hi = jnp.where(second, sin[:, freq], 0.0)
    return cos_t, sin_lo, sin_hi


def _suffix_sum_rhs(n):
    strict_lower = (np.arange(n)[:, None] > np.arange(n)[None, :]).astype(np.float32)
    return np.concatenate([strict_lower, np.ones((n, n), np.float32)], axis=1)


def _row_tile(n, pref):
    return pref if n % pref == 0 else n


def _pages_per_step(n_pages, pref):
    while n_pages % pref:
        pref //= 2
    return pref


def kernel(x_prompt, x_sample, cache_sb_k, cache_sb_v, cache_dsa_k, cache_dsa_v, cache_idx_k, page_table, ffn1_g, ffn1_w1, ffn1_w3, ffn1_w2, mix_g, w_in, w_br_sb, w_br_dsa, w_out, ffn2_g, ffn2_w1, ffn2_w3, ffn2_w2, final_g):
    bsz, seq, d = x_prompt.shape
    nb, n_new, _ = x_sample.shape
    assert n_new == 1, "the sample path is written for one new token per sequence"
    depth = ffn1_g.shape[0]
    n_sb = cache_sb_k.shape[3]
    n_kv = cache_dsa_k.shape[3]
    w_sb = n_sb * HEAD_DIM
    w_dq = w_br_dsa.shape[1]
    n_dsa = w_dq // HEAD_DIM
    wkv = n_kv * HEAD_DIM
    n_idx = (w_in.shape[2] - 3 * w_sb - w_dq - 2 * wkv - IDX_DIM - 2 * d) // (IDX_DIM + 1)
    w_iq = n_idx * IDX_DIM
    assert w_sb == w_dq == w_iq and 2 * wkv + 2 * LANES <= w_sb
    n_pages = page_table.shape[1]
    past = n_pages * PAGE_SIZE
    topk_p = min(TOPK_MAX, seq // 4)
    topk_s = min(TOPK_MAX, (past + n_new) // 4)

    pos_p = jnp.arange(seq)
    pos_s = jnp.full((nb,), past, jnp.int32)
    tabs_p = _rope_tables(pos_p, HEAD_DIM) + _rope_tables(pos_p, IDX_DIM)
    tabs_s = _rope_tables(pos_s, HEAD_DIM) + _rope_tables(pos_s, IDX_DIM)
    tq_sb = _row_tile(seq, SB_BLOCK)
    sum_rhs_q = jnp.asarray(_suffix_sum_rhs(tq_sb), BF16)
    sum_rhs_page = jnp.asarray(_suffix_sum_rhs(PAGE_SIZE), BF16)

    xp = x_prompt.reshape(bsz * seq, d)
    xs = x_sample.reshape(nb, d)
    tm_p = _row_tile(bsz * seq, ROWS_FFN)
    tm_in = _row_tile(seq, ROWS_INPROJ)
    tm_merge = _row_tile(bsz * seq, ROWS_MERGE)
    tf = _row_tile(ffn1_w1.shape[2], FFN_HIDDEN_TILE)
    st_p = [[] for _ in range(5)]
    st_s = [[] for _ in range(5)]

    hp = _rms_call(xp, ffn1_g[0], tm_p)
    hs = _rms_call(xs, ffn1_g[0], nb)
    for layer in range(depth):
        bf = lambda w: w[layer].astype(BF16)
        w1a, w3a, w2a = bf(ffn1_w1), bf(ffn1_w3), bf(ffn1_w2)
        w1b, w3b, w2b = bf(ffn2_w1), bf(ffn2_w3), bf(ffn2_w2)
        wl = w_in[layer]
        off = np.cumsum([0, w_sb, w_sb, w_sb, w_dq, wkv, wkv, w_iq, IDX_DIM, n_idx, d, d])
        col = lambda k: wl[:, off[k]:off[k + 1]]
        pad = lambda w: jnp.zeros((d, w), F32)
        misc = jnp.concatenate([col(4), col(5), col(7), pad(LANES - IDX_DIM), col(8), pad(LANES - n_idx),
                                pad(w_sb - 2 * wkv - 2 * LANES)], axis=1)
        w_tiles = jnp.stack([col(0), col(1), col(2), col(3), col(6), misc]).astype(BF16)
        w_gates = wl[:, off[9]:off[11]].astype(BF16)
        wa, wb, wo = bf(w_br_sb), bf(w_br_dsa), bf(w_out)

        xp, hp = _ffn_call(xp, hp, w1a, w3a, w2a, mix_g[layer], final=False, tm=tm_p, tf=tf)
        xs, hs = _ffn_call(xs, hs, w1a, w3a, w2a, mix_g[layer], final=False, tm=nb, tf=tf)

        (qa, ka, kab, va, vab, qb_t, qi_t, kb, kbb, vb, vb_t, ki, kib, wi_t) = _inproj_call(
            hp, w_tiles, tabs_p, tm=tm_in, n_tab_blocks=seq // tm_in, wkv=wkv, n_idx_heads=n_idx,
            query_on_lanes=True)
        gates = _gates_call(hp, w_gates, tm=tm_p, tn=_row_tile(2 * d, GATE_COLS))
        oa = _sb_prompt_call(qa, kab, vab, sum_rhs_q, bsz=bsz, seq=seq, n_heads=n_sb, tq=tq_sb)
        ob = _dsa_prompt_call(qi_t, wi_t, kib, qb_t, kbb, vb_t, bsz=bsz, seq=seq, n_idx_heads=n_idx,
                              n_heads=n_dsa, n_kv=n_kv, topk=topk_p, tq=_row_tile(seq, DSA_QUERY_BLOCK))
        xp, hp = _merge_call(xp, oa, ob, gates, wa, wb, wo, ffn2_g[layer], tm=tm_merge)
        for store, val, shape in zip(st_p, (ka, va, kb, vb, ki),
                                     ((n_sb, HEAD_DIM), (n_sb, HEAD_DIM), (n_kv, HEAD_DIM), (n_kv, HEAD_DIM), (IDX_DIM,))):
            store.append(val.reshape((bsz, seq) + shape))

        (qa, ka, kab, va, vab, qb, qi, kb, kbb, vb, vbb, ki, kib, wi) = _inproj_call(
            hs, w_tiles, tabs_s, tm=nb, n_tab_blocks=1, wkv=wkv, n_idx_heads=n_idx, query_on_lanes=False)
        gates = _gates_call(hs, w_gates, tm=nb, tn=_row_tile(2 * d, GATE_COLS))
        oa = _sb_sample_call(page_table, qa, cache_sb_k, cache_sb_v, sum_rhs_page, layer=layer)
        keys, meta = _idx_sample_call(page_table, qi, wi, kib, cache_idx_k, layer=layer, topk=topk_s)
        ob = _dsa_sample_call(page_table, keys, meta, qb, kbb, vbb, cache_dsa_k, cache_dsa_v, layer=layer,
                              pages_per_step=_pages_per_step(n_pages, DSA_PAGES))
        xs, hs = _merge_call(xs, oa, ob, gates, wa, wb, wo, ffn2_g[layer], tm=nb)
        for store, val, shape in zip(st_s, (ka, va, kb, vb, ki),
                                     ((n_sb, HEAD_DIM), (n_sb, HEAD_DIM), (n_kv, HEAD_DIM), (n_kv, HEAD_DIM), (IDX_DIM,))):
            store.append(val.reshape((nb, n_new) + shape))

        last = layer == depth - 1
        g_next = final_g if last else ffn1_g[layer + 1]
        if last:
            xp = _ffn_call(xp, hp, w1b, w3b, w2b, g_next, final=True, tm=tm_p, tf=tf)
            xs = _ffn_call(xs, hs, w1b, w3b, w2b, g_next, final=True, tm=nb, tf=tf)
        else:
            xp, hp = _ffn_call(xp, hp, w1b, w3b, w2b, g_next, final=False, tm=tm_p, tf=tf)
            xs, hs = _ffn_call(xs, hs, w1b, w3b, w2b, g_next, final=False, tm=nb, tf=tf)

    return (xp.reshape(bsz, seq, d), xs.reshape(nb, n_new, d),
            jnp.stack(st_p[0]), jnp.stack(st_p[1]), jnp.stack(st_p[2]), jnp.stack(st_p[3]), jnp.stack(st_p[4]),
            jnp.stack(st_s[0]), jnp.stack(st_s[1]), jnp.stack(st_s[2]), jnp.stack(st_s[3]), jnp.stack(st_s[4]))
```

```python
import functools

import jax
import jax.numpy as jnp
import numpy as np
from jax import lax
from jax.experimental import pallas as pl
from jax.experimental.pallas import tpu as pltpu

HEAD_DIM = 128
IDX_DIM = 64
TOPK_MAX = 256
ROPE_THETA = 500000.0
ROT_FRACTION = 4
EPS = 1e-6
PAGE_SIZE = 128
SCALE = HEAD_DIM ** -0.5
LOG2_E = 1.4426950408889634

LANES = 128
SUBLANES = 8
VMEM_LIMIT = 56 * 1024 * 1024

INT_MIN = -(2 ** 31)
NEG_KEY = -2139095041
SOFTMAX_M_INIT = -1e30
MASK_BIAS = -2e30
SB_LOG_UNDERFLOW = -110.0
DSA_KEY_BLOCK = 512

ROWS_FFN = 512
FFN_HIDDEN_TILE = 512
ROWS_INPROJ = 512
ROWS_MERGE = 256
GATE_COLS = 1024
SB_BLOCK = 128
DSA_QUERY_BLOCK = 128
IDX_PAGES = 32
DSA_PAGES = 16

F32 = jnp.float32
BF16 = jnp.bfloat16
I32 = jnp.int32

_NT = (((1,), (1,)), ((), ()))


def _params(sem):
    return pltpu.CompilerParams(dimension_semantics=sem, vmem_limit_bytes=VMEM_LIMIT)


def _rms(x, g):
    return x * lax.rsqrt(jnp.mean(x * x, axis=-1, keepdims=True) + EPS) * g


def _log_sigmoid(z):
    return jnp.minimum(z, 0.0) - jnp.log(1.0 + jnp.exp(-jnp.abs(z)))


def _order_key(x):
    bits = pltpu.bitcast(x, I32)
    bits = jnp.where(bits == INT_MIN, 0, bits)
    return jnp.where(bits < 0, bits ^ 0x7FFFFFFF, bits)


def _split_bf16(x):
    hi = x.astype(BF16)
    lo = (x - hi.astype(F32)).astype(BF16)
    return hi, lo


def _rms_kernel(x_ref, g_ref, h_ref):
    h_ref[...] = _rms(x_ref[...], g_ref[...]).astype(BF16)


def _rms_call(x, g, tm):
    n, d = x.shape
    return pl.pallas_call(
        _rms_kernel,
        grid=(n // tm,),
        in_specs=[pl.BlockSpec((tm, d), lambda i: (i, 0)), pl.BlockSpec((1, d), lambda i: (0, 0))],
        out_specs=pl.BlockSpec((tm, d), lambda i: (i, 0)),
        out_shape=jax.ShapeDtypeStruct((n, d), BF16),
        compiler_params=_params(("parallel",)),
    )(x, g.reshape(1, d))


def _ffn_kernel(x_ref, h_ref, w1_ref, w3_ref, w2_ref, g_ref, *rest, final):
    if final:
        y_ref, acc_ref = rest
    else:
        y_ref, hn_ref, acc_ref = rest
    f = pl.program_id(1)

    @pl.when(f == 0)
    def _():
        acc_ref[...] = jnp.zeros_like(acc_ref)

    h = h_ref[...]
    a = jnp.dot(h, w1_ref[...], preferred_element_type=F32)
    b = jnp.dot(h, w3_ref[...], preferred_element_type=F32)
    u = (a * jax.nn.sigmoid(a)) * b
    acc_ref[...] += jnp.dot(u.astype(BF16), w2_ref[...], preferred_element_type=F32)

    @pl.when(f == pl.num_programs(1) - 1)
    def _():
        y = x_ref[...] + 0.5 * acc_ref[...]
        if final:
            y_ref[...] = _rms(y, g_ref[...])
        else:
            y_ref[...] = y
            hn_ref[...] = _rms(y, g_ref[...]).astype(BF16)


def _ffn_call(x, h, w1, w3, w2, g_next, *, final, tm, tf):
    n, d = x.shape
    dff = w1.shape[1]
    out_shape = [jax.ShapeDtypeStruct((n, d), F32)]
    out_specs = [pl.BlockSpec((tm, d), lambda i, f: (i, 0))]
    if not final:
        out_shape.append(jax.ShapeDtypeStruct((n, d), BF16))
        out_specs.append(pl.BlockSpec((tm, d), lambda i, f: (i, 0)))
    res = pl.pallas_call(
        functools.partial(_ffn_kernel, final=final),
        grid=(n // tm, dff // tf),
        in_specs=[
            pl.BlockSpec((tm, d), lambda i, f: (i, 0)),
            pl.BlockSpec((tm, d), lambda i, f: (i, 0)),
            pl.BlockSpec((d, tf), lambda i, f: (0, f)),
            pl.BlockSpec((d, tf), lambda i, f: (0, f)),
            pl.BlockSpec((tf, d), lambda i, f: (f, 0)),
            pl.BlockSpec((1, d), lambda i, f: (0, 0)),
        ],
        out_specs=out_specs,
        out_shape=out_shape,
        scratch_shapes=[pltpu.VMEM((tm, d), F32)],
        compiler_params=_params(("parallel", "arbitrary")),
    )(x, h, w1, w3, w2, g_next.reshape(1, d))
    return res[0] if final else res


def _rope(x, cos, sin_lo, sin_hi, half):
    return (x * cos + pltpu.roll(x, LANES - half, axis=1) * sin_lo
            + pltpu.roll(x, half, axis=1) * sin_hi)


def _inproj_kernel(h_ref, w_ref, cd_ref, sdl_ref, sdh_ref, ci_ref, sil_ref, sih_ref,
                   qa_ref, ka_ref, kab_ref, va_ref, vab_ref, qb_ref, qi_ref,
                   kb_ref, kbb_ref, vb_ref, vbb_ref, ki_ref, kib_ref, wi_ref, *, wkv, query_on_lanes):
    j = pl.program_id(1)
    acc = jnp.dot(h_ref[...], w_ref[0], preferred_element_type=F32)
    lay = (lambda x: x.T) if query_on_lanes else (lambda x: x)
    rot_d = HEAD_DIM // ROT_FRACTION // 2
    rot_i = IDX_DIM // ROT_FRACTION // 2

    def rope_cols(x, idx):
        if idx:
            c, sl, sh, half = ci_ref[...], sil_ref[...], sih_ref[...], rot_i
        else:
            c, sl, sh, half = cd_ref[...], sdl_ref[...], sdh_ref[...], rot_d
        return jnp.concatenate(
            [_rope(x[:, s:s + LANES], c, sl, sh, half) for s in range(0, x.shape[1], LANES)], axis=1)

    @pl.when(j == 0)
    def _():
        qa_ref[...] = acc.astype(BF16)

    @pl.when(j == 1)
    def _():
        ka_ref[...] = acc
        kab_ref[...] = acc.astype(BF16)

    @pl.when(j == 2)
    def _():
        va_ref[...] = acc
        vab_ref[...] = acc.astype(BF16)

    @pl.when(j == 3)
    def _():
        qb_ref[...] = lay(rope_cols(acc, False)).astype(BF16)

    @pl.when(j == 4)
    def _():
        qi_ref[...] = lay(rope_cols(acc, True)).astype(BF16)

    @pl.when(j == 5)
    def _():
        kb = rope_cols(acc[:, :wkv], False)
        kb_ref[...] = kb
        kbb_ref[...] = kb.astype(BF16)
        vb = acc[:, wkv:2 * wkv]
        vb_ref[...] = vb
        ki = rope_cols(acc[:, 2 * wkv:2 * wkv + LANES], True)[:, :IDX_DIM]
        ki_ref[...] = ki
        kib_ref[...] = ki.astype(BF16)
        wi_lanes = acc[:, 2 * wkv + LANES:2 * wkv + 2 * LANES]
        if query_on_lanes:
            vb_t = vb.T.astype(BF16)
            for c in range(vbb_ref.shape[0]):
                vbb_ref[c] = vb_t[:, c * vbb_ref.shape[2]:(c + 1) * vbb_ref.shape[2]]
            wi_ref[...] = wi_lanes.T[:wi_ref.shape[0], :]
        else:
            vbb_ref[...] = vb.astype(BF16)
            wi_ref[...] = wi_lanes[:, :wi_ref.shape[1]]


def _inproj_call(h, w_tiles, tabs, *, tm, n_tab_blocks, wkv, n_idx_heads, query_on_lanes):
    n, d = h.shape
    nt, _, tw = w_tiles.shape
    row = lambda i, j: (i, 0)
    tab = lambda i, j: (i % n_tab_blocks, 0)
    rows_major = lambda w, dt: (jax.ShapeDtypeStruct((n, w), dt), pl.BlockSpec((tm, w), row))
    if query_on_lanes:
        lanes_major = lambda w, dt: (jax.ShapeDtypeStruct((w, n), dt), pl.BlockSpec((w, tm), lambda i, j: (0, i)))
        v_blocks = (jax.ShapeDtypeStruct((n // DSA_KEY_BLOCK, wkv, DSA_KEY_BLOCK), BF16),
                    pl.BlockSpec((tm // DSA_KEY_BLOCK, wkv, DSA_KEY_BLOCK), lambda i, j: (i, 0, 0)))
    else:
        lanes_major = rows_major
        v_blocks = rows_major(wkv, BF16)
    outs = [rows_major(tw, BF16), rows_major(tw, F32), rows_major(tw, BF16), rows_major(tw, F32),
            rows_major(tw, BF16), lanes_major(tw, BF16), lanes_major(tw, BF16),
            rows_major(wkv, F32), rows_major(wkv, BF16), rows_major(wkv, F32), v_blocks,
            rows_major(IDX_DIM, F32), rows_major(IDX_DIM, BF16), lanes_major(n_idx_heads, F32)]
    return pl.pallas_call(
        functools.partial(_inproj_kernel, wkv=wkv, query_on_lanes=query_on_lanes),
        grid=(n // tm, nt),
        in_specs=[pl.BlockSpec((tm, d), row), pl.BlockSpec((1, d, tw), lambda i, j: (j, 0, 0))]
        + [pl.BlockSpec((tm, LANES), tab)] * 6,
        out_specs=[spec for _, spec in outs],
        out_shape=[shape for shape, _ in outs],
        compiler_params=_params(("parallel", "arbitrary")),
    )(h, w_tiles, *tabs)


def _gates_kernel(h_ref, w_ref, o_ref):
    o_ref[...] = jax.nn.sigmoid(jnp.dot(h_ref[...], w_ref[...], preferred_element_type=F32))


def _gates_call(h, wg, *, tm, tn):
    n, d = h.shape
    ncol = wg.shape[1]
    return pl.pallas_call(
        _gates_kernel,
        grid=(ncol // tn, n // tm),
        in_specs=[pl.BlockSpec((tm, d), lambda j, i: (i, 0)), pl.BlockSpec((d, tn), lambda j, i: (0, j))],
        out_specs=pl.BlockSpec((tm, tn), lambda j, i: (i, j)),
        out_shape=jax.ShapeDtypeStruct((n, ncol), F32),
        compiler_params=_params(("parallel", "arbitrary")),
    )(h, wg)


def _sb_weights(z, sum_rhs, carry, mask):
    n = z.shape[1]
    ls = _log_sigmoid(z)
    lr = ls - z
    if mask is not None:
        lr = jnp.where(mask, lr, 0.0)
    hi, lo = _split_bf16(lr)
    sums = (jnp.dot(hi, sum_rhs, preferred_element_type=F32)
            + jnp.dot(lo, sum_rhs, preferred_element_type=F32))
    a = jnp.exp(ls + sums[:, :n] + carry)
    if mask is not None:
        a = jnp.where(mask, a, 0.0)
    return a, carry + sums[:, n:]


def _sb_prompt_kernel(q_ref, k_ref, v_ref, rhs_ref, o_ref, carry_ref, acc_ref, *, tq, n_heads):
    qi = pl.program_id(1)
    rhs = rhs_ref[...]
    rows = lax.broadcasted_iota(I32, (n_heads * tq, tq), 0) % tq
    cols = lax.broadcasted_iota(I32, (n_heads * tq, tq), 1)
    head_lanes = [slice(h * HEAD_DIM, (h + 1) * HEAD_DIM) for h in range(n_heads)]

    def block(kb, mask, first):
        ks = pl.multiple_of(kb * tq, tq)
        z = jnp.concatenate(
            [lax.dot_general(q_ref[:, hl], k_ref[pl.ds(ks, tq), hl], _NT, preferred_element_type=F32)
             for hl in head_lanes], axis=0) * SCALE
        a, carry = _sb_weights(z, rhs, 0.0 if first else carry_ref[...], mask)
        carry_ref[...] = carry
        ab = a.astype(BF16)
        pv = jnp.concatenate(
            [jnp.dot(ab[h * tq:(h + 1) * tq], v_ref[pl.ds(ks, tq), hl], preferred_element_type=F32)
             for h, hl in enumerate(head_lanes)], axis=0)
        acc_ref[...] = pv if first else acc_ref[...] + pv
        return jnp.max(carry)

    def more(state):
        return (state[0] >= 0) & (state[1] > SB_LOG_UNDERFLOW)

    def step(state):
        return state[0] - 1, block(state[0], None, False)

    lax.while_loop(more, step, (qi - 1, block(qi, cols < rows, True)))
    for h, hl in enumerate(head_lanes):
        o_ref[:, hl] = acc_ref[h * tq:(h + 1) * tq, :].astype(BF16)


def _sb_prompt_call(qa, ka, va, sum_rhs, *, bsz, seq, n_heads, tq):
    width = n_heads * HEAD_DIM
    q3 = qa.reshape(bsz, seq, width)
    k3 = ka.reshape(bsz, seq, width)
    v3 = va.reshape(bsz, seq, width)
    qspec = pl.BlockSpec((None, tq, width), lambda b, i: (b, i, 0))
    kspec = pl.BlockSpec((None, seq, width), lambda b, i: (b, 0, 0), pipeline_mode=pl.Buffered(1))
    out = pl.pallas_call(
        functools.partial(_sb_prompt_kernel, tq=tq, n_heads=n_heads),
        grid=(bsz, seq // tq),
        in_specs=[qspec, kspec, kspec, pl.BlockSpec((tq, 2 * tq), lambda b, i: (0, 0))],
        out_specs=qspec,
        out_shape=jax.ShapeDtypeStruct((bsz, seq, width), BF16),
        scratch_shapes=[pltpu.VMEM((n_heads * tq, tq), F32), pltpu.VMEM((n_heads * tq, HEAD_DIM), F32)],
        compiler_params=_params(("parallel", "arbitrary")),
    )(q3, k3, v3, sum_rhs)
    return out.reshape(bsz * seq, width)


def _kth_largest_key(count_ge, shape, topk):
    def body(it, pat):
        cand = pat ^ jnp.left_shift(jnp.int32(1), 31 - it)
        return jnp.where(count_ge(cand) >= topk, cand, pat)

    return lax.fori_loop(0, 32, body, jnp.full(shape, INT_MIN, I32))


def _tie_cut(count_eq_before, need, shape, nbits):
    def body(it, c):
        cand = c | jnp.left_shift(jnp.int32(1), nbits - 1 - it)
        return jnp.where(count_eq_before(cand) < need, cand, c)

    return lax.fori_loop(0, nbits, body, jnp.zeros(shape, I32)) + 1


def _dsa_prompt_kernel(qi_ref, wi_ref, ki_ref, qb_ref, kb_ref, vb_ref, o_ref,
                       key_ref, m_ref, l_ref, acc_ref, *, tq, tk, topk, n_idx_heads, n_kv, rep, pos_bits):
    i = pl.program_id(1)
    q0 = i * tq
    nkb = (q0 + tq + tk - 1) // tk
    kpos = lax.broadcasted_iota(I32, (tk, tq), 0)
    qpos = q0 + lax.broadcasted_iota(I32, (tk, tq), 1)

    pair_rhs = [jnp.concatenate([qi_ref[h * IDX_DIM:(h + 1) * IDX_DIM, :],
                                 qi_ref[(h + 1) * IDX_DIM:(h + 2) * IDX_DIM, :]], axis=1)
                for h in range(0, n_idx_heads, 2)]
    w_row = [wi_ref[h:h + 1, :] for h in range(n_idx_heads)]

    def score_block(kb, _):
        ks = pl.multiple_of(kb * tk, tk)
        kblk = ki_ref[pl.ds(ks, tk), :]
        sc = jnp.zeros((tk, tq), F32)
        for p, rhs in enumerate(pair_rhs):
            logit = jnp.dot(kblk, rhs, preferred_element_type=F32)
            sc = (sc + jnp.maximum(logit[:, :tq], 0.0) * w_row[2 * p]
                  + jnp.maximum(logit[:, tq:], 0.0) * w_row[2 * p + 1])
        sc = jnp.where(ks + kpos <= qpos, sc, -jnp.inf)
        key_ref[pl.ds(ks, tk), :] = _order_key(sc)
        return 0

    lax.fori_loop(0, nkb, score_block, 0)

    def count(pred):
        def body(kb, cnt):
            ks = pl.multiple_of(kb * tk, tk)
            hit = pred(key_ref[pl.ds(ks, tk), :], ks + kpos).astype(I32)
            return cnt + jnp.sum(hit.reshape(tk // SUBLANES, SUBLANES, tq), axis=0)

        cnt = lax.fori_loop(0, nkb, body, jnp.zeros((SUBLANES, tq), I32))
        return jnp.sum(cnt, axis=0, keepdims=True)

    count_ge = lambda v: count(lambda key, pos: key >= v)
    kth = _kth_largest_key(count_ge, (1, tq), topk)
    thr = jnp.maximum(kth, NEG_KEY + 1)

    n_ge = count_ge(thr)

    @pl.when(jnp.max(n_ge) > topk)
    def _():
        need = topk - count(lambda key, pos: key > thr)
        cut = _tie_cut(lambda c: count(lambda key, pos: (key == thr) & (pos < c)), need, (1, tq), pos_bits)

        def drop(kb, _):
            ks = pl.multiple_of(kb * tk, tk)
            key = key_ref[pl.ds(ks, tk), :]
            key_ref[pl.ds(ks, tk), :] = jnp.where((key == thr) & (ks + kpos >= cut), NEG_KEY, key)
            return 0

        lax.fori_loop(0, nkb, drop, 0)

    m_ref[...] = jnp.full(m_ref.shape, SOFTMAX_M_INIT, F32)
    l_ref[...] = jnp.zeros_like(l_ref)
    acc_ref[...] = jnp.zeros_like(acc_ref)
    q_cols = [jnp.concatenate([qb_ref[(g * rep + r) * HEAD_DIM:(g * rep + r + 1) * HEAD_DIM, :]
                               for r in range(rep)], axis=1) for g in range(n_kv)]

    def attend(kb, _):
        ks = pl.multiple_of(kb * tk, tk)
        bias = jnp.where(key_ref[pl.ds(ks, tk), :] >= thr, 0.0, MASK_BIAS)
        bias = jnp.concatenate([bias] * rep, axis=1)
        s = [jnp.dot(kb_ref[pl.ds(ks, tk), g * HEAD_DIM:(g + 1) * HEAD_DIM], q_cols[g],
                     preferred_element_type=F32) * (SCALE * LOG2_E) + bias for g in range(n_kv)]
        m_old = [m_ref[g] for g in range(n_kv)]
        m_new = [jnp.maximum(m_old[g], jnp.max(s[g], axis=0, keepdims=True)) for g in range(n_kv)]
        p = [jnp.exp2(s[g] - m_new[g]) for g in range(n_kv)]
        for g in range(n_kv):
            alpha = jnp.exp2(m_old[g] - m_new[g])
            l_ref[g] = alpha * l_ref[g] + jnp.sum(p[g], axis=0, keepdims=True)
            v_t = vb_ref[kb, g * HEAD_DIM:(g + 1) * HEAD_DIM, :]
            acc_ref[g] = alpha * acc_ref[g] + jnp.dot(v_t, p[g].astype(BF16), preferred_element_type=F32)
            m_ref[g] = m_new[g]
        return 0

    lax.fori_loop(0, nkb, attend, 0)
    for g in range(n_kv):
        out_t = acc_ref[g] / l_ref[g]
        for r in range(rep):
            h = g * rep + r
            o_ref[:, h * HEAD_DIM:(h + 1) * HEAD_DIM] = out_t[:, r * tq:(r + 1) * tq].T.astype(BF16)


def _dsa_prompt_call(qi_t, wi_t, ki, qb_t, kb, vb_t, *, bsz, seq, n_idx_heads, n_heads, n_kv, topk, tq):
    rep = n_heads // n_kv
    nq = seq // tq
    tk = vb_t.shape[2]
    wkv = n_kv * HEAD_DIM
    wq, wb = n_idx_heads * IDX_DIM, n_heads * HEAD_DIM
    qspec = lambda w: pl.BlockSpec((w, tq), lambda b, i: (0, b * nq + i))
    kspec = lambda w: pl.BlockSpec((None, seq, w), lambda b, i: (b, 0, 0))
    out = pl.pallas_call(
        functools.partial(_dsa_prompt_kernel, tq=tq, tk=tk, topk=topk, n_idx_heads=n_idx_heads,
                          n_kv=n_kv, rep=rep, pos_bits=max(1, int(seq - 1).bit_length())),
        grid=(bsz, nq),
        in_specs=[qspec(wq), qspec(n_idx_heads), kspec(IDX_DIM), qspec(wb), kspec(wkv),
                  pl.BlockSpec((seq // tk, wkv, tk), lambda b, i: (b, 0, 0))],
        out_specs=pl.BlockSpec((None, tq, wb), lambda b, i: (b, i, 0)),
        out_shape=jax.ShapeDtypeStruct((bsz, seq, wb), BF16),
        scratch_shapes=[pltpu.VMEM((seq, tq), I32),
                        pltpu.VMEM((n_kv, 1, rep * tq), F32),
                        pltpu.VMEM((n_kv, 1, rep * tq), F32),
                        pltpu.VMEM((n_kv, HEAD_DIM, rep * tq), F32)],
        compiler_params=_params(("parallel", "arbitrary")),
    )(qi_t, wi_t, ki.reshape(bsz, seq, IDX_DIM), qb_t, kb.reshape(bsz, seq, wkv), vb_t)
    return out.reshape(bsz * seq, wb)


def _sb_sample_kernel(pt_ref, q_ref, rhs_ref, ck_ref, cv_ref, o_ref, kbuf, vbuf, sem, *, layer, n_heads, n_pages):
    b = pl.program_id(0)
    q = q_ref[...]
    rhs = rhs_ref[...]
    head = lax.broadcasted_iota(I32, (n_heads, PAGE_SIZE), 0)

    def page_copies(n, slot):
        page = pt_ref[b, n_pages - 1 - n]
        return (pltpu.make_async_copy(ck_ref.at[layer, page], kbuf.at[slot], sem.at[0, slot]),
                pltpu.make_async_copy(cv_ref.at[layer, page], vbuf.at[slot], sem.at[1, slot]))

    def start(n, slot):
        for copy in page_copies(n, slot):
            copy.start()

    def wait(n, slot):
        for copy in page_copies(n, slot):
            copy.wait()

    def more(state):
        return (state[0] < n_pages) & (state[1] > SB_LOG_UNDERFLOW)

    def step(state):
        n, _, carry, acc = state
        slot = n % 2
        wait(n, slot)

        @pl.when(n + 1 < n_pages)
        def _():
            start(n + 1, 1 - slot)

        z = jnp.zeros((n_heads, PAGE_SIZE), F32)
        for h in range(n_heads):
            kh = kbuf[slot, pl.ds(h, PAGE_SIZE, stride=n_heads), :].astype(BF16)
            z = jnp.where(head == h, lax.dot_general(q, kh, _NT, preferred_element_type=F32), z)
        a, carry = _sb_weights(z * SCALE, rhs, carry, None)
        ab = a.astype(BF16)
        for h in range(n_heads):
            vh = vbuf[slot, pl.ds(h, PAGE_SIZE, stride=n_heads), :].astype(BF16)
            acc = acc + jnp.where(head == h, jnp.dot(ab, vh, preferred_element_type=F32), 0.0)
        return n + 1, jnp.max(carry), carry, acc

    start(0, 0)
    n_done, _, _, acc = lax.while_loop(
        more, step, (jnp.int32(0), jnp.float32(0.0), jnp.zeros((n_heads, PAGE_SIZE), F32),
                     jnp.zeros((n_heads, HEAD_DIM), F32)))

    @pl.when(n_done < n_pages)
    def _():
        wait(n_done, n_done % 2)

    o_ref[...] = acc.astype(BF16)


def _page_specs(block, n_steps_pages, pages_per_step, layer, reverse):
    specs = []
    for p in range(pages_per_step):
        if reverse:
            pick = lambda b, j, pt, p=p: pt[b, n_steps_pages - 1 - (j * pages_per_step + p)]
        else:
            pick = lambda b, j, pt, p=p: pt[b, j * pages_per_step + p]
        zeros = (0,) * (len(block) - 2)
        specs.append(pl.BlockSpec(block, lambda b, j, pt, pick=pick: (layer, pick(b, j, pt)) + zeros))
    return specs


def _flat_pages(cache):
    depth, pool, page, heads, dim = cache.shape
    return cache.reshape(depth, pool, page * heads, dim)


def _sb_sample_call(page_table, qa, cache_k, cache_v, sum_rhs, *, layer):
    nb, n_pages = page_table.shape
    n_heads = cache_k.shape[3]
    q3 = qa.reshape(nb, n_heads, HEAD_DIM)
    row = pl.BlockSpec((None, n_heads, HEAD_DIM), lambda b, pt: (b, 0, 0))
    page_rows = PAGE_SIZE * n_heads
    out = pl.pallas_call(
        functools.partial(_sb_sample_kernel, layer=layer, n_heads=n_heads, n_pages=n_pages),
        grid_spec=pltpu.PrefetchScalarGridSpec(
            num_scalar_prefetch=1,
            grid=(nb,),
            in_specs=[row, pl.BlockSpec((PAGE_SIZE, 2 * PAGE_SIZE), lambda b, pt: (0, 0)),
                      pl.BlockSpec(memory_space=pl.ANY), pl.BlockSpec(memory_space=pl.ANY)],
            out_specs=row,
            scratch_shapes=[pltpu.VMEM((2, page_rows, HEAD_DIM), F32), pltpu.VMEM((2, page_rows, HEAD_DIM), F32),
                            pltpu.SemaphoreType.DMA((2, 2))],
        ),
        out_shape=jax.ShapeDtypeStruct((nb, n_heads, HEAD_DIM), BF16),
        compiler_params=_params(("arbitrary",)),
    )(page_table, q3, sum_rhs, _flat_pages(cache_k), _flat_pages(cache_v))
    return out.reshape(nb, n_heads * HEAD_DIM)


def _idx_sample_kernel(pt_ref, qi_ref, wi_ref, *rest, pages_per_step):
    del pt_ref
    page_refs = rest[:pages_per_step]
    key_ref = rest[pages_per_step]
    j = pl.program_id(1)
    logit = jnp.concatenate(
        [jnp.dot(qi_ref[...], page_ref[...].astype(BF16), preferred_element_type=F32)
         for page_ref in page_refs], axis=1)
    keys = _order_key(jnp.sum(jnp.maximum(logit, 0.0) * wi_ref[...], axis=0, keepdims=True))
    for p in range(pages_per_step):
        key_ref[pl.ds(j * pages_per_step + p, 1), :] = keys[:, p * PAGE_SIZE:(p + 1) * PAGE_SIZE]


def _idx_select_kernel(keyin_ref, qi_ref, wi_ref, kin_ref, key_ref, meta_ref, *, topk, pos_bits):
    keys = keyin_ref[...]
    nb, n_pages, _ = keys.shape
    past = n_pages * PAGE_SIZE
    pos = (lax.broadcasted_iota(I32, keys.shape, 1) * PAGE_SIZE + lax.broadcasted_iota(I32, keys.shape, 2))
    own = jnp.sum(qi_ref[...].astype(F32) * kin_ref[...].astype(F32), axis=2, keepdims=True)
    own_key = _order_key(jnp.sum(jnp.maximum(own, 0.0) * wi_ref[...], axis=1, keepdims=True))

    def total(x):
        return jnp.sum(jnp.sum(x.astype(I32), axis=2, keepdims=True), axis=1, keepdims=True)

    def count(pred):
        return total(pred(keys, pos)) + pred(own_key, past).astype(I32)

    count_ge = lambda v: count(lambda key, where: key >= v)
    thr = _kth_largest_key(count_ge, (nb, 1, 1), topk)
    need = topk - count(lambda key, where: key > thr)
    cut = _tie_cut(lambda c: count(lambda key, where: (key == thr) & (where < c)), need, (nb, 1, 1), pos_bits)
    key_ref[...] = jnp.where((keys == thr) & (pos >= cut), NEG_KEY, keys)
    own_sel = (own_key > thr) | ((own_key == thr) & (past < cut))
    rowid = lax.broadcasted_iota(I32, meta_ref.shape, 1)
    meta_ref[...] = jnp.where(rowid == 0, thr, own_sel.astype(I32))


def _idx_sample_call(page_table, qi, wi, ki_new, cache_idx, *, layer, pages_per_step, topk):
    nb, n_pages = page_table.shape
    nh = wi.shape[1]
    row = lambda s: pl.BlockSpec((None,) + s, lambda b, j, pt: (b, 0, 0))
    cache_t = jnp.swapaxes(cache_idx, 2, 3)
    pspecs = _page_specs((None, None, IDX_DIM, PAGE_SIZE), n_pages, pages_per_step, layer, False)
    q3, w3 = qi.reshape(nb, nh, IDX_DIM), wi.reshape(nb, nh, 1)
    keys = pl.pallas_call(
        functools.partial(_idx_sample_kernel, pages_per_step=pages_per_step),
        grid_spec=pltpu.PrefetchScalarGridSpec(
            num_scalar_prefetch=1,
            grid=(nb, n_pages // pages_per_step),
            in_specs=[row((nh, IDX_DIM)), row((nh, 1))] + pspecs,
            out_specs=row((n_pages, PAGE_SIZE)),
        ),
        out_shape=jax.ShapeDtypeStruct((nb, n_pages, PAGE_SIZE), I32),
        compiler_params=_params(("parallel", "arbitrary")),
    )(page_table, q3, w3, *([cache_t] * pages_per_step))
    full = lambda s: pl.BlockSpec(s, lambda i: (0,) * len(s))
    return pl.pallas_call(
        functools.partial(_idx_select_kernel, topk=topk, pos_bits=int(n_pages * PAGE_SIZE).bit_length()),
        grid=(1,),
        in_specs=[full((nb, n_pages, PAGE_SIZE)), full((nb, nh, IDX_DIM)), full((nb, nh, 1)), full((nb, 1, IDX_DIM))],
        out_specs=[full((nb, n_pages, PAGE_SIZE)), full((nb, SUBLANES, LANES))],
        out_shape=[jax.ShapeDtypeStruct((nb, n_pages, PAGE_SIZE), I32),
                   jax.ShapeDtypeStruct((nb, SUBLANES, LANES), I32)],
        compiler_params=_params(("arbitrary",)),
    )(keys, q3, w3, ki_new.reshape(nb, 1, IDX_DIM))


def _dsa_sample_kernel(pt_ref, key_ref, meta_ref, q_ref, kn_ref, vn_ref, *rest, n_heads, n_kv, pages_per_step):
    del pt_ref
    k_refs = rest[:pages_per_step]
    v_refs = rest[pages_per_step:2 * pages_per_step]
    o_ref, m_ref, l_ref, acc_ref = rest[2 * pages_per_step:]
    j = pl.program_id(1)
    rep = n_heads // n_kv

    @pl.when(j == 0)
    def _():
        m_ref[...] = jnp.full(m_ref.shape, SOFTMAX_M_INIT, F32)
        l_ref[...] = jnp.zeros_like(l_ref)
        acc_ref[...] = jnp.zeros_like(acc_ref)

    q = q_ref[...]
    meta = meta_ref[...]
    thr = meta[0:1, :]
    group = lax.broadcasted_iota(I32, (n_heads, PAGE_SIZE), 0) // rep
    m, l, acc = m_ref[...], l_ref[...], acc_ref[...]

    def update(m, l, acc, s, pv):
        m_new = jnp.maximum(m, jnp.max(s, axis=1, keepdims=True))
        p = jnp.exp(s - m_new)
        alpha = jnp.exp(m - m_new)
        return m_new, alpha * l + jnp.sum(p, axis=1, keepdims=True), alpha * acc + pv(p)

    step_keys = pages_per_step * PAGE_SIZE
    group_all = lax.broadcasted_iota(I32, (n_heads, step_keys), 0) // rep

    def group_rows(refs, g):
        return jnp.concatenate([r[pl.ds(g, PAGE_SIZE, stride=n_kv), :].astype(BF16) for r in refs], axis=0)

    bias = jnp.concatenate(
        [jnp.where(key_ref[pl.ds(j * pages_per_step + p_i, 1), :] >= thr, 0.0, MASK_BIAS)
         for p_i in range(pages_per_step)], axis=1)
    s_all = jnp.zeros((n_heads, step_keys), F32)
    for g in range(n_kv):
        sg = lax.dot_general(q, group_rows(k_refs, g), _NT, preferred_element_type=F32)
        s_all = jnp.where(group_all == g, sg, s_all)

    def pv(p):
        pb = p.astype(BF16)
        out = jnp.zeros((n_heads, HEAD_DIM), F32)
        for g in range(n_kv):
            og = jnp.dot(pb, group_rows(v_refs, g), preferred_element_type=F32)
            out = jnp.where(group == g, og, out)
        return out

    m, l, acc = update(m, l, acc, s_all * SCALE + bias, pv)
    m_ref[...], l_ref[...], acc_ref[...] = m, l, acc

    @pl.when(j == pl.num_programs(1) - 1)
    def _():
        own_bias = jnp.where(meta[1:2, 0:1] > 0, 0.0, MASK_BIAS)
        s_own = jnp.sum(q.astype(F32) * kn_ref[...].astype(F32), axis=1, keepdims=True) * SCALE + own_bias
        vn = vn_ref[...].astype(F32)
        _, l_f, acc_f = update(m, l, acc, s_own, lambda p: p.astype(BF16).astype(F32) * vn)
        o_ref[...] = (acc_f / l_f).astype(BF16)


def _dsa_sample_call(page_table, keys, meta, qb, kb_new, vb_new, cache_k, cache_v, *, layer, pages_per_step):
    nb, n_pages = page_table.shape
    n_kv = cache_k.shape[3]
    n_heads = qb.shape[1] // HEAD_DIM
    rep = n_heads // n_kv
    row = lambda s: pl.BlockSpec((None,) + s, lambda b, j, pt: (b, 0, 0))
    own = lambda t: jnp.repeat(t.reshape(nb, n_kv, HEAD_DIM), rep, axis=1)
    page_block = (None, None, PAGE_SIZE * n_kv, HEAD_DIM)
    kspecs = _page_specs(page_block, n_pages, pages_per_step, layer, False)
    out = pl.pallas_call(
        functools.partial(_dsa_sample_kernel, n_heads=n_heads, n_kv=n_kv, pages_per_step=pages_per_step),
        grid_spec=pltpu.PrefetchScalarGridSpec(
            num_scalar_prefetch=1,
            grid=(nb, n_pages // pages_per_step),
            in_specs=[row((n_pages, PAGE_SIZE)), row((SUBLANES, LANES)), row((n_heads, HEAD_DIM)),
                      row((n_heads, HEAD_DIM)), row((n_heads, HEAD_DIM))] + kspecs + kspecs,
            out_specs=row((n_heads, HEAD_DIM)),
            scratch_shapes=[pltpu.VMEM((n_heads, 1), F32), pltpu.VMEM((n_heads, 1), F32),
                            pltpu.VMEM((n_heads, HEAD_DIM), F32)],
        ),
        out_shape=jax.ShapeDtypeStruct((nb, n_heads, HEAD_DIM), BF16),
        compiler_params=_params(("parallel", "arbitrary")),
    )(page_table, keys, meta, qb.reshape(nb, n_heads, HEAD_DIM), own(kb_new), own(vb_new),
      *([_flat_pages(cache_k)] * pages_per_step), *([_flat_pages(cache_v)] * pages_per_step))
    return out.reshape(nb, n_heads * HEAD_DIM)


def _merge_kernel(x_ref, oa_ref, ob_ref, ga_ref, gb_ref, wa_ref, wb_ref, wo_ref, g_ref, y_ref, hn_ref):
    mixed = (ga_ref[...] * jnp.dot(oa_ref[...], wa_ref[...], preferred_element_type=F32)
             + gb_ref[...] * jnp.dot(ob_ref[...], wb_ref[...], preferred_element_type=F32))
    y = x_ref[...] + jnp.dot(mixed.astype(BF16), wo_ref[...], preferred_element_type=F32)
    y_ref[...] = y
    hn_ref[...] = _rms(y, g_ref[...]).astype(BF16)


def _merge_call(x, oa, ob, gates, wa, wb, wo, g_next, *, tm):
    n, d = x.shape
    row = lambda w: pl.BlockSpec((tm, w), lambda i: (i, 0))
    const = lambda s: pl.BlockSpec(s, lambda i: (0, 0), pipeline_mode=pl.Buffered(1))
    return pl.pallas_call(
        _merge_kernel,
        grid=(n // tm,),
        in_specs=[row(d), row(oa.shape[1]), row(ob.shape[1]),
                  pl.BlockSpec((tm, d), lambda i: (i, 0)), pl.BlockSpec((tm, d), lambda i: (i, 1)),
                  const(wa.shape), const(wb.shape), const(wo.shape), const((1, d))],
        out_specs=[row(d), row(d)],
        out_shape=[jax.ShapeDtypeStruct((n, d), F32), jax.ShapeDtypeStruct((n, d), BF16)],
        compiler_params=_params(("parallel",)),
    )(x, oa, ob, gates, gates, wa, wb, wo, g_next.reshape(1, d))


def _rope_tables(pos, head_dim):
    rot = head_dim // ROT_FRACTION
    half = rot // 2
    inv = 1.0 / (ROPE_THETA ** (jnp.arange(half, dtype=F32) * 2.0 / rot))
    ang = pos.astype(F32)[:, None] * inv[None, :]
    cos, sin = jnp.cos(ang), jnp.sin(ang)
    lane = np.arange(LANES) % head_dim
    first, second = lane < half, (lane >= half) & (lane < rot)
    freq = np.where(second, lane - half, np.where(first, lane, 0))
    cos_t = jnp.where(first | second, cos[:, freq], 1.0)
    sin_lo = jnp.where(first, -sin[:, freq], 0.0)
    sin_hi = jnp.where(second, sin[:, freq], 0.0)
    return cos_t, sin_lo, sin_hi


def _suffix_sum_rhs(n):
    strict_lower = (np.arange(n)[:, None] > np.arange(n)[None, :]).astype(np.float32)
    return np.concatenate([strict_lower, np.ones((n, n), np.float32)], axis=1)


def _row_tile(n, pref):
    return pref if n % pref == 0 else n


def _pages_per_step(n_pages, pref):
    while n_pages % pref:
        pref //= 2
    return pref


def kernel(x_prompt, x_sample, cache_sb_k, cache_sb_v, cache_dsa_k, cache_dsa_v, cache_idx_k, page_table, ffn1_g, ffn1_w1, ffn1_w3, ffn1_w2, mix_g, w_in, w_br_sb, w_br_dsa, w_out, ffn2_g, ffn2_w1, ffn2_w3, ffn2_w2, final_g):
    bsz, seq, d = x_prompt.shape
    nb, n_new, _ = x_sample.shape
    assert n_new == 1, "the sample path is written for one new token per sequence"
    depth = ffn1_g.shape[0]
    n_sb = cache_sb_k.shape[3]
    n_kv = cache_dsa_k.shape[3]
    w_sb = n_sb * HEAD_DIM
    w_dq = w_br_dsa.shape[1]
    n_dsa = w_dq // HEAD_DIM
    wkv = n_kv * HEAD_DIM
    n_idx = (w_in.shape[2] - 3 * w_sb - w_dq - 2 * wkv - IDX_DIM - 2 * d) // (IDX_DIM + 1)
    w_iq = n_idx * IDX_DIM
    assert w_sb == w_dq == w_iq and 2 * wkv + 2 * LANES <= w_sb
    n_pages = page_table.shape[1]
    past = n_pages * PAGE_SIZE
    topk_p = min(TOPK_MAX, seq // 4)
    topk_s = min(TOPK_MAX, (past + n_new) // 4)

    pos_p = jnp.arange(seq)
    pos_s = jnp.full((nb,), past, jnp.int32)
    tabs_p = _rope_tables(pos_p, HEAD_DIM) + _rope_tables(pos_p, IDX_DIM)
    tabs_s = _rope_tables(pos_s, HEAD_DIM) + _rope_tables(pos_s, IDX_DIM)
    tq_sb = _row_tile(seq, SB_BLOCK)
    sum_rhs_q = jnp.asarray(_suffix_sum_rhs(tq_sb), BF16)
    sum_rhs_page = jnp.asarray(_suffix_sum_rhs(PAGE_SIZE), BF16)

    xp = x_prompt.reshape(bsz * seq, d)
    xs = x_sample.reshape(nb, d)
    tm_p = _row_tile(bsz * seq, ROWS_FFN)
    tm_in = _row_tile(seq, ROWS_INPROJ)
    tm_merge = _row_tile(bsz * seq, ROWS_MERGE)
    tf = _row_tile(ffn1_w1.shape[2], FFN_HIDDEN_TILE)
    st_p = [[] for _ in range(5)]
    st_s = [[] for _ in range(5)]

    hp = _rms_call(xp, ffn1_g[0], tm_p)
    hs = _rms_call(xs, ffn1_g[0], nb)
    for layer in range(depth):
        bf = lambda w: w[layer].astype(BF16)
        w1a, w3a, w2a = bf(ffn1_w1), bf(ffn1_w3), bf(ffn1_w2)
        w1b, w3b, w2b = bf(ffn2_w1), bf(ffn2_w3), bf(ffn2_w2)
        wl = w_in[layer]
        off = np.cumsum([0, w_sb, w_sb, w_sb, w_dq, wkv, wkv, w_iq, IDX_DIM, n_idx, d, d])
        col = lambda k: wl[:, off[k]:off[k + 1]]
        pad = lambda w: jnp.zeros((d, w), F32)
        misc = jnp.concatenate([col(4), col(5), col(7), pad(LANES - IDX_DIM), col(8), pad(LANES - n_idx),
                                pad(w_sb - 2 * wkv - 2 * LANES)], axis=1)
        w_tiles = jnp.stack([col(0), col(1), col(2), col(3), col(6), misc]).astype(BF16)
        w_gates = wl[:, off[9]:off[11]].astype(BF16)
        wa, wb, wo = bf(w_br_sb), bf(w_br_dsa), bf(w_out)

        xp, hp = _ffn_call(xp, hp, w1a, w3a, w2a, mix_g[layer], final=False, tm=tm_p, tf=tf)
        xs, hs = _ffn_call(xs, hs, w1a, w3a, w2a, mix_g[layer], final=False, tm=nb, tf=tf)

        (qa, ka, kab, va, vab, qb_t, qi_t, kb, kbb, vb, vb_t, ki, kib, wi_t) = _inproj_call(
            hp, w_tiles, tabs_p, tm=tm_in, n_tab_blocks=seq // tm_in, wkv=wkv, n_idx_heads=n_idx,
            query_on_lanes=True)
        gates = _gates_call(hp, w_gates, tm=tm_p, tn=_row_tile(2 * d, GATE_COLS))
        oa = _sb_prompt_call(qa, kab, vab, sum_rhs_q, bsz=bsz, seq=seq, n_heads=n_sb, tq=tq_sb)
        ob = _dsa_prompt_call(qi_t, wi_t, kib, qb_t, kbb, vb_t, bsz=bsz, seq=seq, n_idx_heads=n_idx,
                              n_heads=n_dsa, n_kv=n_kv, topk=topk_p, tq=_row_tile(seq, DSA_QUERY_BLOCK))
        xp, hp = _merge_call(xp, oa, ob, gates, wa, wb, wo, ffn2_g[layer], tm=tm_merge)
        for store, val, shape in zip(st_p, (ka, va, kb, vb, ki),
                                     ((n_sb, HEAD_DIM), (n_sb, HEAD_DIM), (n_kv, HEAD_DIM), (n_kv, HEAD_DIM), (IDX_DIM,))):
            store.append(val.reshape((bsz, seq) + shape))

        (qa, ka, kab, va, vab, qb, qi, kb, kbb, vb, vbb, ki, kib, wi) = _inproj_call(
            hs, w_tiles, tabs_s, tm=nb, n_tab_blocks=1, wkv=wkv, n_idx_heads=n_idx, query_on_lanes=False)
        gates = _gates_call(hs, w_gates, tm=nb, tn=_row_tile(2 * d, GATE_COLS))
        oa = _sb_sample_call(page_table, qa, cache_sb_k, cache_sb_v, sum_rhs_page, layer=layer)
        keys, meta = _idx_sample_call(page_table, qi, wi, kib, cache_idx_k, layer=layer,
                                      pages_per_step=_pages_per_step(n_pages, IDX_PAGES), topk=topk_s)
        ob = _dsa_sample_call(page_table, keys, meta, qb, kbb, vbb, cache_dsa_k, cache_dsa_v, layer=layer,
                              pages_per_step=_pages_per_step(n_pages, DSA_PAGES))
        xs, hs = _merge_call(xs, oa, ob, gates, wa, wb, wo, ffn2_g[layer], tm=nb)
        for store, val, shape in zip(st_s, (ka, va, kb, vb, ki),
                                     ((n_sb, HEAD_DIM), (n_sb, HEAD_DIM), (n_kv, HEAD_DIM), (n_kv, HEAD_DIM), (IDX_DIM,))):
            store.append(val.reshape((nb, n_new) + shape))

        last = layer == depth - 1
        g_next = final_g if last else ffn1_g[layer + 1]
        if last:
            xp = _ffn_call(xp, hp, w1b, w3b, w2b, g_next, final=True, tm=tm_p, tf=tf)
            xs = _ffn_call(xs, hs, w1b, w3b, w2b, g_next, final=True, tm=nb, tf=tf)
        else:
            xp, hp = _ffn_call(xp, hp, w1b, w3b, w2b, g_next, final=False, tm=tm_p, tf=tf)
            xs, hs = _ffn_call(xs, hs, w1b, w3b, w2b, g_next, final=False, tm=nb, tf=tf)

    return (xp.reshape(bsz, seq, d), xs.reshape(nb, n_new, d),
            jnp.stack(st_p[0]), jnp.stack(st_p[1]), jnp.stack(st_p[2]), jnp.stack(st_p[3]), jnp.stack(st_p[4]),
            jnp.stack(st_s[0]), jnp.stack(st_s[1]), jnp.stack(st_s[2]), jnp.stack(st_s[3]), jnp.stack(st_s[4]))
```
